```python
import math
import jax, jax.numpy as jnp
from jax import lax
import numpy as np

D_MODEL = 1024
BATCH = 8
SEQ = 4096
DEPTH = 2

MEM_LEN = 256
Q_BLOCK = 128
RMS_EPS = 1e-6
SSM_WIDTH = D_MODEL // 2
SSM_GROUP = 16
SSM_GROUPS = SSM_WIDTH // SSM_GROUP
SSM_STATE = 64
SB_HEADS = 8
SB_HEAD_DIM = 64
SB_WIDTH = SB_HEADS * SB_HEAD_DIM
DA_HEADS = 4
DA_QK_DIM = 64
DA_V_DIM = 2 * DA_QK_DIM
DA_QK_WIDTH = DA_HEADS * 2 * DA_QK_DIM
DA_WIDTH = DA_HEADS * DA_V_DIM
ROT_DIM = DA_QK_DIM // 4
ROPE_THETA = 500000.0
N_BRANCH = 3
BRANCH_WIDTH = 512
XA_HEADS = 4
XA_HEAD_DIM = 128
XA_WIDTH = XA_HEADS * XA_HEAD_DIM
MLP_HIDDEN = 4 * D_MODEL
IN_SPLITS = (SSM_WIDTH, SB_WIDTH, SB_WIDTH, SB_WIDTH, DA_QK_WIDTH, DA_QK_WIDTH, DA_WIDTH, N_BRANCH * D_MODEL)
N_IN = sum(IN_SPLITS)

kernel_name = 'hybrid_s5_stickbreak_diffattn_gated'


def rmsnorm(x, gain):
    x32 = x.astype(jnp.float32)
    y = x32 * lax.rsqrt(jnp.mean(x32 * x32, axis=-1, keepdims=True) + RMS_EPS)
    return (y * gain.astype(jnp.float32)).astype(x.dtype)


def rope_partial(t, cos, sin):
    half = ROT_DIM // 2
    t1 = t[..., :half]
    t2 = t[..., half:ROT_DIM]
    return jnp.concatenate([t1 * cos - t2 * sin, t2 * cos + t1 * sin, t[..., ROT_DIM:]], axis=-1)


def query_blocks(fn, q):
    b, h, s, d = q.shape
    nb = s // Q_BLOCK
    qb = q.reshape(b, h, nb, Q_BLOCK, d).transpose(2, 0, 1, 3, 4)
    starts = jnp.arange(nb, dtype=jnp.int32) * Q_BLOCK
    out = lax.map(lambda a: fn(a[0], a[1]), (qb, starts))
    return out.transpose(1, 2, 0, 3, 4).reshape(b, h, s, out.shape[-1])


def stick_breaking_attention(q, k, v):
    scale = q.shape[-1] ** -0.5
    key_idx = jnp.arange(k.shape[2])

    def blk(qb, t0):
        q_idx = t0 + jnp.arange(Q_BLOCK)
        mask = key_idx[None, :] < q_idx[:, None]
        z = jnp.einsum('bhqd,bhkd->bhqk', qb, k).astype(jnp.float32) * scale
        log_1mb = jnp.where(mask, jax.nn.log_sigmoid(-z), 0.0)
        later = lax.cumsum(log_1mb, axis=3, reverse=True) - log_1mb
        w = jnp.where(mask, jnp.exp(jax.nn.log_sigmoid(z) + later), 0.0)
        return jnp.einsum('bhqk,bhkd->bhqd', w.astype(v.dtype), v)

    return query_blocks(blk, q)


def differential_attention(q, k, v, lam):
    dk = q.shape[-1] // 2
    scale = dk ** -0.5
    k1 = k[..., :dk]
    k2 = k[..., dk:]
    key_idx = jnp.arange(k.shape[2])

    def blk(qb, t0):
        q_idx = t0 + jnp.arange(Q_BLOCK)
        mask = key_idx[None, :] <= q_idx[:, None]
        s1 = jnp.einsum('bhqd,bhkd->bhqk', qb[..., :dk], k1).astype(jnp.float32) * scale
        s2 = jnp.einsum('bhqd,bhkd->bhqk', qb[..., dk:], k2).astype(jnp.float32) * scale
        p1 = jax.nn.softmax(jnp.where(mask, s1, -jnp.inf), axis=-1)
        p2 = jax.nn.softmax(jnp.where(mask, s2, -jnp.inf), axis=-1)
        w = p1 - lam * p2
        return jnp.einsum('bhqk,bhkd->bhqd', w.astype(v.dtype), v)

    return query_blocks(blk, q)


def _ssm_combine(e1, e2):
    a1, b1 = e1
    a2, b2 = e2
    return a2 * a1, a2 * b1 + b2


def s5_ssm(u, lam_re, lam_im, log_dt, b_re, b_im, c_re, c_im, d_skip):
    bsz, seq, _ = u.shape
    ug = u.astype(jnp.float32).reshape(bsz, seq, SSM_GROUPS, SSM_GROUP)
    lam = lax.complex(lam_re.astype(jnp.float32), lam_im.astype(jnp.float32))
    dt = jnp.exp(log_dt.astype(jnp.float32))[:, None]
    lam_bar = jnp.exp(lam * dt)
    bmat = lax.complex(b_re.astype(jnp.float32), b_im.astype(jnp.float32))
    b_bar = ((lam_bar - 1.0) / lam)[..., None] * bmat
    bu = jnp.einsum('gpc,blgc->blgp', b_bar, ug.astype(jnp.complex64))
    a = jnp.broadcast_to(lam_bar, bu.shape)
    _, states = lax.associative_scan(_ssm_combine, (a, bu), axis=1)
    y = (jnp.einsum('gcp,blgp->blgc', c_re.astype(jnp.float32), states.real)
         - jnp.einsum('gcp,blgp->blgc', c_im.astype(jnp.float32), states.imag))
    y = y + d_skip.astype(jnp.float32).reshape(SSM_GROUPS, SSM_GROUP) * ug
    return y.reshape(bsz, seq, SSM_WIDTH).astype(u.dtype)


def _normal(k, shape, scale):
    return jax.random.normal(k, shape, jnp.float32) * scale


def setup_inputs(seed: int = 0) -> dict:
    key = jax.random.key(seed)
    ks = jax.random.split(key, 32)
    gain = lambda k, shape: 1.0 + _normal(k, shape, 0.02)
    offsets = jax.random.randint(ks[2], (BATCH, 1), 0, 1024, dtype=jnp.int32)
    positions = offsets + jnp.arange(SEQ, dtype=jnp.int32)[None, :]
    lam_im = jnp.pi * jnp.arange(SSM_STATE, dtype=jnp.float32)[None, None, :] + _normal(ks[7], (DEPTH, SSM_GROUPS, SSM_STATE), 0.01)
    return {
        'x': _normal(ks[0], (BATCH, SEQ, D_MODEL), 1.0),
        'mem': _normal(ks[1], (BATCH, MEM_LEN, D_MODEL), 1.0),
        'positions': positions,
        'norm_mix': gain(ks[3], (DEPTH, D_MODEL)),
        'w_in': _normal(ks[4], (DEPTH, D_MODEL, N_IN), D_MODEL ** -0.5),
        'ssm_lam_re': -0.5 + _normal(ks[6], (DEPTH, SSM_GROUPS, SSM_STATE), 0.01),
        'ssm_lam_im': lam_im,
        'ssm_log_dt': jax.random.uniform(ks[8], (DEPTH, SSM_GROUPS), jnp.float32, math.log(1e-3), math.log(1e-1)),
        'ssm_b_re': _normal(ks[9], (DEPTH, SSM_GROUPS, SSM_STATE, SSM_GROUP), (2 * SSM_GROUP) ** -0.5),
        'ssm_b_im': _normal(ks[10], (DEPTH, SSM_GROUPS, SSM_STATE, SSM_GROUP), (2 * SSM_GROUP) ** -0.5),
        'ssm_c_re': _normal(ks[11], (DEPTH, SSM_GROUPS, SSM_GROUP, SSM_STATE), (2 * SSM_STATE) ** -0.5),
        'ssm_c_im': _normal(ks[12], (DEPTH, SSM_GROUPS, SSM_GROUP, SSM_STATE), (2 * SSM_STATE) ** -0.5),
        'ssm_d': _normal(ks[13], (DEPTH, SSM_WIDTH), 1.0),
        'ssm_w_glu': _normal(ks[14], (DEPTH, SSM_WIDTH, 2 * SSM_WIDTH), SSM_WIDTH ** -0.5),
        'diff_lambda': _normal(ks[15], (DEPTH, 4, DA_QK_DIM), 0.1),
        'diff_subln': gain(ks[16], (DEPTH, DA_V_DIM)),
        'w_branch': _normal(ks[17], (DEPTH, N_BRANCH, BRANCH_WIDTH, D_MODEL), BRANCH_WIDTH ** -0.5),
        'w_out': _normal(ks[18], (DEPTH, D_MODEL, D_MODEL), D_MODEL ** -0.5),
        'norm_cross': gain(ks[19], (DEPTH, D_MODEL)),
        'norm_mem': gain(ks[20], (DEPTH, D_MODEL)),
        'w_xq': _normal(ks[21], (DEPTH, D_MODEL, XA_WIDTH), D_MODEL ** -0.5),
        'w_xkv': _normal(ks[22], (DEPTH, D_MODEL, 2 * XA_WIDTH), D_MODEL ** -0.5),
        'w_xo': _normal(ks[23], (DEPTH, XA_WIDTH, D_MODEL), XA_WIDTH ** -0.5),
        'norm_mlp': gain(ks[24], (DEPTH, D_MODEL)),
        'w_up': _normal(ks[25], (DEPTH, D_MODEL, MLP_HIDDEN), D_MODEL ** -0.5),
        'w_down': _normal(ks[26], (DEPTH, MLP_HIDDEN, D_MODEL), MLP_HIDDEN ** -0.5),
        'norm_final': gain(ks[27], (D_MODEL,)),
    }


def reference(x, mem, positions, norm_mix, w_in, ssm_lam_re, ssm_lam_im, ssm_log_dt, ssm_b_re, ssm_b_im,
              ssm_c_re, ssm_c_im, ssm_d, ssm_w_glu, diff_lambda, diff_subln, w_branch, w_out,
              norm_cross, norm_mem, w_xq, w_xkv, w_xo, norm_mlp, w_up, w_down, norm_final):
    bsz, seq, _ = x.shape
    split_idx = np.cumsum(IN_SPLITS)[:-1].tolist()

    inv_freq = ROPE_THETA ** (-jnp.arange(0, ROT_DIM, 2, dtype=jnp.float32) / ROT_DIM)
    ang = positions.astype(jnp.float32)[..., None] * inv_freq
    cos = jnp.cos(ang)[:, :, None, None, :].astype(x.dtype)
    sin = jnp.sin(ang)[:, :, None, None, :].astype(x.dtype)

    def heads(t, n):
        return t.reshape(bsz, seq, n, -1).transpose(0, 2, 1, 3)

    def merge_heads(t):
        return t.transpose(0, 2, 1, 3).reshape(bsz, seq, -1)

    for l in range(DEPTH):
        h = rmsnorm(x, norm_mix[l])
        u, sq, sk, sv, dq, dk, dv, gate_logits = jnp.split(h @ w_in[l], split_idx, axis=-1)

        y_ssm = jax.nn.gelu(s5_ssm(u, ssm_lam_re[l], ssm_lam_im[l], ssm_log_dt[l], ssm_b_re[l], ssm_b_im[l],
                                   ssm_c_re[l], ssm_c_im[l], ssm_d[l]))
        glu_a, glu_b = jnp.split(y_ssm @ ssm_w_glu[l], 2, axis=-1)
        y_ssm = glu_a * jax.nn.sigmoid(glu_b)

        y_sb = merge_heads(stick_breaking_attention(heads(sq, SB_HEADS), heads(sk, SB_HEADS), heads(sv, SB_HEADS)))

        dq_r = rope_partial(dq.reshape(bsz, seq, DA_HEADS, 2, DA_QK_DIM), cos, sin)
        dk_r = rope_partial(dk.reshape(bsz, seq, DA_HEADS, 2, DA_QK_DIM), cos, sin)
        dq_r = dq_r.reshape(bsz, seq, DA_HEADS, 2 * DA_QK_DIM).transpose(0, 2, 1, 3)
        dk_r = dk_r.reshape(bsz, seq, DA_HEADS, 2 * DA_QK_DIM).transpose(0, 2, 1, 3)
        lam_vec = diff_lambda[l].astype(jnp.float32)
        lam_init = 0.8 - 0.6 * math.exp(-0.3 * l)
        lam = (jnp.exp(jnp.sum(lam_vec[0] * lam_vec[1])) - jnp.exp(jnp.sum(lam_vec[2] * lam_vec[3]))
               + lam_init)
        o = differential_attention(dq_r, dk_r, heads(dv, DA_HEADS), lam)
        o = rmsnorm(o, diff_subln[l]) * (1.0 - lam_init)
        y_da = merge_heads(o)

        branches = jnp.stack([y_ssm, y_sb, y_da], axis=2)
        proj = jnp.einsum('bsnc,ncd->bsnd', branches, w_branch[l])
        gates = jax.nn.sigmoid(gate_logits.reshape(bsz, seq, N_BRANCH, D_MODEL))
        x = x + jnp.sum(gates * proj, axis=2) @ w_out[l]

        hx = rmsnorm(x, norm_cross[l])
        m = rmsnorm(mem, norm_mem[l])
        xq = hx.reshape(bsz, seq, D_MODEL) @ w_xq[l]
        xq = xq.reshape(bsz, seq, XA_HEADS, XA_HEAD_DIM)
        xk, xv = jnp.split(m @ w_xkv[l], 2, axis=-1)
        xk = xk.reshape(bsz, -1, XA_HEADS, XA_HEAD_DIM)
        xv = xv.reshape(bsz, -1, XA_HEADS, XA_HEAD_DIM)
        sc = jnp.einsum('bshd,bmhd->bhsm', xq, xk).astype(jnp.float32) * (XA_HEAD_DIM ** -0.5)
        p = jax.nn.softmax(sc, axis=-1).astype(xv.dtype)
        xo = jnp.einsum('bhsm,bmhd->bshd', p, xv).reshape(bsz, seq, XA_WIDTH)
        x = x + xo @ w_xo[l]

        hm = rmsnorm(x, norm_mlp[l])
        x = x + jnp.square(jax.nn.relu(hm @ w_up[l])) @ w_down[l]

    return rmsnorm(x, norm_final)
```

```python
import functools
import math

import jax
import jax.numpy as jnp
from jax import lax
from jax.experimental import pallas as pl
from jax.experimental.pallas import tpu as pltpu

F32 = jnp.float32
BF16 = jnp.bfloat16

D_MODEL = 1024
MEM_LEN = 256
RMS_EPS = 1e-6
SSM_WIDTH = 512
SSM_GROUP = 16
SSM_GROUPS = 32
SSM_STATE = 64
SB_HEADS = 8
SB_HEAD_DIM = 64
SB_WIDTH = 512
DA_HEADS = 4
DA_QK_DIM = 64
DA_V_DIM = 128
DA_QK_WIDTH = 512
DA_WIDTH = 512
ROT_DIM = 16
ROPE_THETA = 500000.0
N_BRANCH = 3
BRANCH_WIDTH = 512
XA_HEADS = 4
XA_HEAD_DIM = 128
XA_WIDTH = 512
MLP_HIDDEN = 4096

LANES = 128
SUBLANES = 8
VMEM_LIMIT_BYTES = 58 * 1024 * 1024

N_PROJ_A = SSM_WIDTH + 3 * SB_WIDTH + 2 * DA_QK_WIDTH + DA_WIDTH
N_QKV = N_PROJ_A - SSM_WIDTH
SBQ_BLK, SBK_BLK, SBV_BLK = 0, 4, 8
DAQ_BLK, DAK_BLK, DAV_BLK = 12, 16, 20

SSM_WIN = SSM_WIDTH // LANES
SSM_WIN_STATE = (SSM_GROUPS // SSM_WIN) * SSM_STATE

SB_LOG_CUTOFF = -104.0
MASK_VALUE = -1e30


def _rms(x, gain):
    ms = jnp.mean(x * x, axis=-1, keepdims=True)
    return x * lax.rsqrt(ms + RMS_EPS) * gain


def _dot(a, b):
    return jnp.dot(a, b, preferred_element_type=F32)


def _dot_nt(a, b):
    return lax.dot_general(a, b, (((1,), (1,)), ((), ())), preferred_element_type=F32)


def _params(*sem):
    return pltpu.CompilerParams(dimension_semantics=sem, vmem_limit_bytes=VMEM_LIMIT_BYTES)


def _const_spec(shape):
    nd = len(shape)
    return pl.BlockSpec(shape, lambda *_: (0,) * nd, pipeline_mode=pl.Buffered(1))


def _memkv_kernel(mem_ref, g_ref, w_ref, o_ref):
    h = _rms(mem_ref[0], g_ref[0]).astype(BF16)
    o_ref[0, 0] = _dot(h, w_ref[0]).astype(BF16)


def _memkv(mem, norm_mem, w_xkv_bf):
    depth = w_xkv_bf.shape[0]
    bsz = mem.shape[0]
    return pl.pallas_call(
        _memkv_kernel,
        grid=(depth, bsz),
        in_specs=[
            pl.BlockSpec((1, MEM_LEN, D_MODEL), lambda l, b: (b, 0, 0)),
            pl.BlockSpec((1, 1, D_MODEL), lambda l, b: (l, 0, 0)),
            pl.BlockSpec((1, D_MODEL, 2 * XA_WIDTH), lambda l, b: (l, 0, 0)),
        ],
        out_specs=pl.BlockSpec((1, 1, MEM_LEN, 2 * XA_WIDTH), lambda l, b: (l, b, 0, 0)),
        out_shape=jax.ShapeDtypeStruct((depth, bsz, MEM_LEN, 2 * XA_WIDTH), BF16),
        compiler_params=_params("arbitrary", "arbitrary"),
        name="memkv",
    )(mem, norm_mem.reshape(depth, 1, D_MODEL), w_xkv_bf)


def _inproj_ssm_kernel(x_ref, g_ref, wa_ref, cos_ref, sina_ref, sinb_ref, bw_ref, lr_ref, li_ref,
                       cw_ref, d_ref, wglu_ref, qkv_ref, yssm_ref, u_tb, bux, st, y_tb, *, ts, bsz):
    @pl.when(pl.program_id(0) == 0)
    def _():
        st[...] = jnp.zeros_like(st)

    gain = g_ref[...]
    for b in range(bsz):
        h = _rms(x_ref[b], gain).astype(BF16)
        p = _dot(h, wa_ref[...])
        for w in range(SSM_WIN):
            u_tb[w, pl.ds(b, ts, stride=bsz), :] = p[:, LANES * w:LANES * (w + 1)]
        o = SSM_WIDTH
        qkv_ref[b, :, 0:SB_WIDTH] = (p[:, o:o + SB_WIDTH] * SB_HEAD_DIM ** -0.5).astype(BF16)
        qkv_ref[b, :, SB_WIDTH:3 * SB_WIDTH] = p[:, o + SB_WIDTH:o + 3 * SB_WIDTH].astype(BF16)
        cosf, sina, sinb = cos_ref[b], sina_ref[b], sinb_ref[b]
        src = o + 3 * SB_WIDTH
        dst = 3 * SB_WIDTH
        for scale in (DA_QK_DIM ** -0.5, 1.0):
            for hh in range(DA_HEADS):
                t = p[:, src + LANES * hh:src + LANES * (hh + 1)]
                r = (t * cosf + pltpu.roll(t, ROT_DIM // 2, 1) * sina
                     + pltpu.roll(t, LANES - ROT_DIM // 2, 1) * sinb)
                qkv_ref[b, :, dst + LANES * hh:dst + LANES * (hh + 1)] = (r * scale).astype(BF16)
            src += DA_QK_WIDTH
            dst += DA_QK_WIDTH
        qkv_ref[b, :, dst:dst + DA_WIDTH] = p[:, src:src + DA_WIDTH].astype(BF16)

    wst = 2 * SSM_WIN_STATE
    for w in range(SSM_WIN):
        bux[:, wst * w:wst * (w + 1)] = _dot(u_tb[w].astype(BF16), bw_ref[w])

    for w in range(SSM_WIN):
        c0 = wst * w
        c1 = c0 + SSM_WIN_STATE
        c2 = c0 + wst
        lr = lr_ref[:, SSM_WIN_STATE * w:SSM_WIN_STATE * (w + 1)]
        li = li_ref[:, SSM_WIN_STATE * w:SSM_WIN_STATE * (w + 1)]

        def body(t, carry, c0=c0, c1=c1, c2=c2, lr=lr, li=li):
            xr, xi = carry
            r = pl.multiple_of(t * bsz, bsz)
            nxr = lr * xr - li * xi + bux[pl.ds(r, bsz), c0:c1]
            nxi = lr * xi + li * xr + bux[pl.ds(r, bsz), c1:c2]
            bux[pl.ds(r, bsz), c0:c1] = nxr
            bux[pl.ds(r, bsz), c1:c2] = nxi
            return nxr, nxi

        xr, xi = lax.fori_loop(0, ts, body, (st[:, c0:c1], st[:, c1:c2]), unroll=2)
        st[:, c0:c1] = xr
        st[:, c1:c2] = xi

    ys = []
    for w in range(SSM_WIN):
        xw = bux[:, wst * w:wst * (w + 1)].astype(BF16)
        yw = _dot(xw, cw_ref[w])
        yw = yw + d_ref[:, LANES * w:LANES * (w + 1)] * u_tb[w]
        ys.append(jax.nn.gelu(yw).astype(BF16))
    glu = _dot(jnp.concatenate(ys, axis=1), wglu_ref[...])
    for w in range(SSM_WIN):
        sl = slice(LANES * w, LANES * (w + 1))
        y_tb[w] = glu[:, sl] * jax.nn.sigmoid(glu[:, SSM_WIDTH + LANES * w:SSM_WIDTH + LANES * (w + 1)])
    for b in range(bsz):
        for w in range(SSM_WIN):
            yssm_ref[b, :, LANES * w:LANES * (w + 1)] = y_tb[w, pl.ds(b, ts, stride=bsz), :].astype(BF16)


def _inproj_ssm(x, gain, wa, rope, ssm, d_skip, w_glu, *, ts):
    bsz, seq, _ = x.shape
    assert bsz == SUBLANES and seq % ts == 0
    cosf, sina, sinb = rope
    bw, lr, li, cw = ssm
    m = ts * bsz
    tile = lambda width: pl.BlockSpec((bsz, ts, width), lambda i: (0, i, 0))
    return pl.pallas_call(
        functools.partial(_inproj_ssm_kernel, ts=ts, bsz=bsz),
        grid=(seq // ts,),
        in_specs=[
            tile(D_MODEL),
            _const_spec((1, D_MODEL)),
            _const_spec((D_MODEL, N_PROJ_A)),
            tile(LANES), tile(LANES), tile(LANES),
            _const_spec(bw.shape), _const_spec(lr.shape), _const_spec(li.shape), _const_spec(cw.shape),
            _const_spec((1, SSM_WIDTH)),
            _const_spec((SSM_WIDTH, 2 * SSM_WIDTH)),
        ],
        out_specs=[tile(N_QKV), tile(SSM_WIDTH)],
        out_shape=[jax.ShapeDtypeStruct((bsz, seq, N_QKV), BF16),
                   jax.ShapeDtypeStruct((bsz, seq, SSM_WIDTH), BF16)],
        scratch_shapes=[
            pltpu.VMEM((SSM_WIN, m, LANES), F32),
            pltpu.VMEM((m, 2 * SSM_WIN * SSM_WIN_STATE), F32),
            pltpu.VMEM((bsz, 2 * SSM_WIN * SSM_WIN_STATE), F32),
            pltpu.VMEM((SSM_WIN, m, LANES), F32),
        ],
        compiler_params=_params("arbitrary"),
        name="inproj_ssm",
    )(x, gain.reshape(1, D_MODEL), wa, cosf, sina, sinb, bw, lr, li, cw,
      d_skip.reshape(1, SSM_WIDTH), w_glu)


def _sb_kernel(q_ref, k_ref, v_ref, o_ref, *, tq):
    i = pl.program_id(2)
    q2 = q_ref[0]
    lane = lax.broadcasted_iota(jnp.int32, (tq, LANES), 1)
    row = lax.broadcasted_iota(jnp.int32, (tq, tq), 0)
    col = lax.broadcasted_iota(jnp.int32, (tq, tq), 1)
    strict = col < row
    tri = (row > col).astype(BF16)

    def block(qa, j, masked, run):
        ks = pl.multiple_of(j * tq, tq)
        kb = k_ref[0, pl.ds(ks, tq), :]
        vb = v_ref[0, pl.ds(ks, tq), :]
        z = _dot_nt(qa, kb)
        l1m = -(jnp.maximum(z, 0.0) + jnp.log(1.0 + jnp.exp(-jnp.abs(z))))
        if masked:
            l1m = jnp.where(strict, l1m, 0.0)
        hi = l1m.astype(BF16)
        lo = (l1m - hi.astype(F32)).astype(BF16)
        later = _dot(hi, tri) + _dot(lo, tri) + run
        w = jnp.exp(z + l1m + later)
        if masked:
            w = jnp.where(strict, w, 0.0)
        pv = _dot(w.astype(BF16), vb)
        return pv, run + jnp.sum(l1m, axis=1, keepdims=True)

    accs = []
    for a in range(2):
        qa = jnp.where((lane >= SB_HEAD_DIM * a) & (lane < SB_HEAD_DIM * (a + 1)), q2, jnp.zeros_like(q2))
        acc0, run0 = block(qa, i, True, jnp.zeros((tq, 1), F32))

        def cond(c):
            j, run, _ = c
            return jnp.logical_and(j >= 0, jnp.max(run) > SB_LOG_CUTOFF)

        def body(c, qa=qa):
            j, run, acc = c
            pv, run = block(qa, j, False, run)
            return j - 1, run, acc + pv

        _, _, acc = lax.while_loop(cond, body, (i - 1, run0, acc0))
        accs.append(acc)
    o_ref[0] = jnp.where(lane < SB_HEAD_DIM, accs[0], accs[1]).astype(BF16)


def _sb_attention(qkv, *, tq):
    bsz, seq, _ = qkv.shape
    npair = SB_WIDTH // LANES
    return pl.pallas_call(
        functools.partial(_sb_kernel, tq=tq),
        grid=(bsz, npair, seq // tq),
        in_specs=[
            pl.BlockSpec((1, tq, LANES), lambda b, p, i: (b, i, SBQ_BLK + p)),
            pl.BlockSpec((1, seq, LANES), lambda b, p, i: (b, 0, SBK_BLK + p)),
            pl.BlockSpec((1, seq, LANES), lambda b, p, i: (b, 0, SBV_BLK + p)),
        ],
        out_specs=pl.BlockSpec((1, tq, LANES), lambda b, p, i: (b, i, p)),
        out_shape=jax.ShapeDtypeStruct((bsz, seq, SB_WIDTH), BF16),
        compiler_params=_params("parallel", "parallel", "arbitrary"),
        name="sb_attn",
    )(qkv, qkv, qkv)


def _da_kernel(q_ref, k_ref, v_ref, lam_ref, g_ref, o_ref, *, tq, lam_init):
    i = pl.program_id(2)
    q = q_ref[0]
    lane = lax.broadcasted_iota(jnp.int32, (tq, LANES), 1)
    zero = jnp.zeros_like(q)
    qs = (jnp.where(lane < DA_QK_DIM, q, zero), jnp.where(lane >= DA_QK_DIM, q, zero))
    row = lax.broadcasted_iota(jnp.int32, (tq, tq), 0)
    col = lax.broadcasted_iota(jnp.int32, (tq, tq), 1)
    causal = col <= row

    def step(j, masked, carry):
        ks = pl.multiple_of(j * tq, tq)
        kb = k_ref[0, pl.ds(ks, tq), :]
        vb = v_ref[0, pl.ds(ks, tq), :]
        out = []
        for c in range(2):
            m, l, acc = carry[c]
            s = _dot_nt(qs[c], kb)
            if masked:
                s = jnp.where(causal, s, MASK_VALUE)
            mn = jnp.maximum(m, jnp.max(s, axis=1, keepdims=True))
            alpha = jnp.exp(m - mn)
            p = jnp.exp(s - mn)
            l = alpha * l + jnp.sum(p, axis=1, keepdims=True)
            acc = alpha * acc + _dot(p.astype(BF16), vb)
            out.append((mn, l, acc))
        return tuple(out)

    init = tuple((jnp.full((tq, 1), MASK_VALUE, F32), jnp.zeros((tq, 1), F32), jnp.zeros((tq, LANES), F32))
                 for _ in range(2))
    carry = lax.fori_loop(0, i, lambda j, c: step(j, False, c), init)
    (_, l1, a1), (_, l2, a2) = step(i, True, carry)

    lv = lam_ref[...]
    lam = (jnp.exp(jnp.sum(lv[0:1] * lv[1:2], axis=1, keepdims=True))
           - jnp.exp(jnp.sum(lv[2:3] * lv[3:4], axis=1, keepdims=True)) + lam_init)
    o = a1 / l1 - lam * (a2 / l2)
    o_ref[0] = (_rms(o, g_ref[...]) * (1.0 - lam_init)).astype(BF16)


def _da_attention(qkv, diff_lambda, diff_subln, lam_init, *, tq):
    bsz, seq, _ = qkv.shape
    return pl.pallas_call(
        functools.partial(_da_kernel, tq=tq, lam_init=lam_init),
        grid=(bsz, DA_HEADS, seq // tq),
        in_specs=[
            pl.BlockSpec((1, tq, LANES), lambda b, h, i: (b, i, DAQ_BLK + h)),
            pl.BlockSpec((1, seq, LANES), lambda b, h, i: (b, 0, DAK_BLK + h)),
            pl.BlockSpec((1, seq, LANES), lambda b, h, i: (b, 0, DAV_BLK + h)),
            pl.BlockSpec((4, DA_QK_DIM), lambda b, h, i: (0, 0)),
            pl.BlockSpec((1, DA_V_DIM), lambda b, h, i: (0, 0)),
        ],
        out_specs=pl.BlockSpec((1, tq, LANES), lambda b, h, i: (b, i, h)),
        out_shape=jax.ShapeDtypeStruct((bsz, seq, DA_WIDTH), BF16),
        compiler_params=_params("parallel", "parallel", "arbitrary"),
        name="da_attn",
    )(qkv, qkv, qkv, diff_lambda, diff_subln.reshape(1, DA_V_DIM))


def _merge_cross_kernel(x_ref, yssm_ref, ysb_ref, yda_ref, gmix_ref, wg_ref, wbr_ref, wout_ref,
                        gcross_ref, wxq_ref, kv_ref, wxo_ref, o_ref):
    x = x_ref[0]
    h = _rms(x, gmix_ref[...]).astype(BF16)
    merged = None
    for n, y_ref in enumerate((yssm_ref, ysb_ref, yda_ref)):
        gate = jax.nn.sigmoid(_dot(h, wg_ref[:, D_MODEL * n:D_MODEL * (n + 1)]))
        term = gate * _dot(y_ref[0], wbr_ref[n])
        merged = term if merged is None else merged + term
    x1 = x + _dot(merged.astype(BF16), wout_ref[...])

    hx = _rms(x1, gcross_ref[...]).astype(BF16)
    xq = _dot(hx, wxq_ref[...]).astype(BF16)
    kv = kv_ref[0]
    heads = []
    for hh in range(XA_HEADS):
        sl = slice(XA_HEAD_DIM * hh, XA_HEAD_DIM * (hh + 1))
        s = _dot_nt(xq[:, sl], kv[:, sl]) * XA_HEAD_DIM ** -0.5
        p = jnp.exp(s - jnp.max(s, axis=1, keepdims=True))
        vh = kv[:, XA_WIDTH + XA_HEAD_DIM * hh:XA_WIDTH + XA_HEAD_DIM * (hh + 1)]
        heads.append((_dot(p.astype(BF16), vh) / jnp.sum(p, axis=1, keepdims=True)).astype(BF16))
    o_ref[0] = x1 + _dot(jnp.concatenate(heads, axis=1), wxo_ref[...])


def _merge_cross(x, yssm, ysb, yda, gmix, wg, wbr, wout, gcross, wxq, kv, wxo, *, tm):
    bsz, seq, _ = x.shape
    tile = lambda width: pl.BlockSpec((1, tm, width), lambda b, i: (b, i, 0))
    return pl.pallas_call(
        _merge_cross_kernel,
        grid=(bsz, seq // tm),
        in_specs=[
            tile(D_MODEL), tile(BRANCH_WIDTH), tile(BRANCH_WIDTH), tile(BRANCH_WIDTH),
            _const_spec((1, D_MODEL)),
            _const_spec((D_MODEL, N_BRANCH * D_MODEL)),
            _const_spec((N_BRANCH, BRANCH_WIDTH, D_MODEL)),
            _const_spec((D_MODEL, D_MODEL)),
            _const_spec((1, D_MODEL)),
            _const_spec((D_MODEL, XA_WIDTH)),
            pl.BlockSpec((1, MEM_LEN, 2 * XA_WIDTH), lambda b, i: (b, 0, 0)),
            _const_spec((XA_WIDTH, D_MODEL)),
        ],
        out_specs=tile(D_MODEL),
        out_shape=jax.ShapeDtypeStruct((bsz, seq, D_MODEL), F32),
        compiler_params=_params("parallel", "arbitrary"),
        name="merge_cross",
    )(x, yssm, ysb, yda, gmix.reshape(1, D_MODEL), wg, wbr, wout, gcross.reshape(1, D_MODEL), wxq, kv, wxo)


def _mlp_kernel(x_ref, g_ref, wup_ref, wdown_ref, gfin_ref, o_ref, *, final, chunk):
    x = x_ref[...]
    h = _rms(x, g_ref[...]).astype(BF16)
    acc = x
    for c in range(MLP_HIDDEN // chunk):
        up = _dot(h, wup_ref[:, chunk * c:chunk * (c + 1)])
        act = jnp.square(jnp.maximum(up, 0.0)).astype(BF16)
        acc = acc + _dot(act, wdown_ref[chunk * c:chunk * (c + 1), :])
    if final:
        acc = _rms(acc, gfin_ref[...])
    o_ref[...] = acc


def _mlp(x2d, gain, wup, wdown, gfin, *, final, tm):
    n = x2d.shape[0]
    tile = pl.BlockSpec((tm, D_MODEL), lambda i: (i, 0))
    return pl.pallas_call(
        functools.partial(_mlp_kernel, final=final, chunk=1024),
        grid=(n // tm,),
        in_specs=[
            tile,
            _const_spec((1, D_MODEL)),
            _const_spec((D_MODEL, MLP_HIDDEN)),
            _const_spec((MLP_HIDDEN, D_MODEL)),
            _const_spec((1, D_MODEL)),
        ],
        out_specs=tile,
        out_shape=jax.ShapeDtypeStruct((n, D_MODEL), F32),
        compiler_params=_params("parallel"),
        name="mlp",
    )(x2d, gain.reshape(1, D_MODEL), wup, wdown, gfin.reshape(1, D_MODEL))


def _ssm_tables(lam_re, lam_im, log_dt, b_re, b_im, c_re, c_im, bsz):
    dt = jnp.exp(log_dt)[:, None]
    mag = jnp.exp(lam_re * dt)
    lbr = mag * jnp.cos(lam_im * dt)
    lbi = mag * jnp.sin(lam_im * dt)
    den = lam_re * lam_re + lam_im * lam_im
    fr = ((lbr - 1.0) * lam_re + lbi * lam_im) / den
    fi = (lbi * lam_re - (lbr - 1.0) * lam_im) / den
    bbr = fr[..., None] * b_re - fi[..., None] * b_im
    bbi = fr[..., None] * b_im + fi[..., None] * b_re
    gl = SSM_GROUPS // SSM_WIN
    eye = jnp.eye(gl, dtype=F32)
    shp = (SSM_WIN, gl, SSM_STATE, SSM_GROUP)
    to_b = lambda t: jnp.einsum("wgpc,gh->wgchp", t.reshape(shp), eye)
    bw = jnp.stack([to_b(bbr), to_b(bbi)], axis=3).reshape(SSM_WIN, LANES, 2 * SSM_WIN_STATE)
    shc = (SSM_WIN, gl, SSM_GROUP, SSM_STATE)
    to_c = lambda t: jnp.einsum("wgcp,gh->wgphc", t.reshape(shc), eye)
    cw = jnp.stack([to_c(c_re), -to_c(c_im)], axis=1).reshape(SSM_WIN, 2 * SSM_WIN_STATE, LANES)
    lr = jnp.broadcast_to(lbr.reshape(1, -1), (bsz, SSM_GROUPS * SSM_STATE))
    li = jnp.broadcast_to(lbi.reshape(1, -1), (bsz, SSM_GROUPS * SSM_STATE))
    return bw.astype(BF16), lr, li, cw.astype(BF16)


def _rope_tables(positions):
    half = ROT_DIM // 2
    inv_freq = ROPE_THETA ** (-jnp.arange(0, ROT_DIM, 2, dtype=F32) / ROT_DIM)
    ang = positions.astype(F32)[..., None] * inv_freq
    cos, sin = jnp.cos(ang), jnp.sin(ang)
    lead = ang.shape[:-1]
    ones = jnp.ones(lead + (DA_QK_DIM - ROT_DIM,), F32)
    zeros = jnp.zeros(lead + (DA_QK_DIM - ROT_DIM,), F32)
    zh = jnp.zeros(lead + (half,), F32)
    rep = lambda t: jnp.concatenate([t, t], axis=-1)
    cosf = rep(jnp.concatenate([cos, cos, ones], axis=-1))
    sina = rep(jnp.concatenate([zh, sin, zeros], axis=-1))
    sinb = rep(jnp.concatenate([-sin, zh, zeros], axis=-1))
    return cosf, sina, sinb


def kernel(x, mem, positions, norm_mix, w_in, ssm_lam_re, ssm_lam_im, ssm_log_dt, ssm_b_re, ssm_b_im, ssm_c_re, ssm_c_im, ssm_d, ssm_w_glu, diff_lambda, diff_subln, w_branch, w_out, norm_cross, norm_mem, w_xq, w_xkv, w_xo, norm_mlp, w_up, w_down, norm_final):
    bsz, seq, _ = x.shape
    depth = w_in.shape[0]
    ts = min(64, seq)
    tq = min(128, seq)
    tm = min(512, seq)

    rope = _rope_tables(positions)
    kv_all = _memkv(mem, norm_mem, w_xkv.astype(BF16))

    for l in range(depth):
        wa = w_in[l, :, :N_PROJ_A].astype(BF16)
        wg = w_in[l, :, N_PROJ_A:].astype(BF16)
        ssm = _ssm_tables(ssm_lam_re[l], ssm_lam_im[l], ssm_log_dt[l], ssm_b_re[l], ssm_b_im[l],
                          ssm_c_re[l], ssm_c_im[l], bsz)
        qkv, yssm = _inproj_ssm(x, norm_mix[l], wa, rope, ssm, ssm_d[l], ssm_w_glu[l].astype(BF16), ts=ts)
        ysb = _sb_attention(qkv, tq=tq)
        lam_init = 0.8 - 0.6 * math.exp(-0.3 * l)
        yda = _da_attention(qkv, diff_lambda[l], diff_subln[l], lam_init, tq=tq)
        x = _merge_cross(x, yssm, ysb, yda, norm_mix[l], wg, w_branch[l].astype(BF16), w_out[l].astype(BF16),
                         norm_cross[l], w_xq[l].astype(BF16), kv_all[l], w_xo[l].astype(BF16), tm=tm)
        x = _mlp(x.reshape(bsz * seq, D_MODEL), norm_mlp[l], w_up[l].astype(BF16), w_down[l].astype(BF16),
                 norm_final, final=(l == depth - 1), tm=tm).reshape(bsz, seq, D_MODEL)
    return x
```

```python
import functools
import math

import jax
import jax.numpy as jnp
from jax import lax
from jax.experimental import pallas as pl
from jax.experimental.pallas import tpu as pltpu

F32 = jnp.float32
BF16 = jnp.bfloat16

D_MODEL = 1024
MEM_LEN = 256
RMS_EPS = 1e-6
SSM_WIDTH = 512
SSM_GROUP = 16
SSM_GROUPS = 32
SSM_STATE = 64
SB_HEADS = 8
SB_HEAD_DIM = 64
SB_WIDTH = 512
DA_HEADS = 4
DA_QK_DIM = 64
DA_V_DIM = 128
DA_QK_WIDTH = 512
DA_WIDTH = 512
ROT_DIM = 16
ROPE_THETA = 500000.0
N_BRANCH = 3
BRANCH_WIDTH = 512
XA_HEADS = 4
XA_HEAD_DIM = 128
XA_WIDTH = 512
MLP_HIDDEN = 4096

LANES = 128
SUBLANES = 8
VMEM_LIMIT_BYTES = 58 * 1024 * 1024

N_PROJ_A = SSM_WIDTH + 3 * SB_WIDTH + 2 * DA_QK_WIDTH + DA_WIDTH
N_QKV = N_PROJ_A - SSM_WIDTH
SBQ_BLK, SBK_BLK, SBV_BLK = 0, 4, 8
DAQ_BLK, DAK_BLK, DAV_BLK = 12, 16, 20

SSM_WIN = SSM_WIDTH // LANES
SSM_WIN_STATE = (SSM_GROUPS // SSM_WIN) * SSM_STATE

SB_LOG_CUTOFF = -104.0
MASK_VALUE = -1e30


def _rms(x, gain):
    ms = jnp.mean(x * x, axis=-1, keepdims=True)
    return x * lax.rsqrt(ms + RMS_EPS) * gain


def _dot(a, b):
    return jnp.dot(a, b, preferred_element_type=F32)


def _dot_nt(a, b):
    return lax.dot_general(a, b, (((1,), (1,)), ((), ())), preferred_element_type=F32)


def _params(*sem):
    return pltpu.CompilerParams(dimension_semantics=sem, vmem_limit_bytes=VMEM_LIMIT_BYTES)


def _const_spec(shape):
    nd = len(shape)
    return pl.BlockSpec(shape, lambda *_: (0,) * nd, pipeline_mode=pl.Buffered(1))


def _memkv_kernel(mem_ref, g_ref, w_ref, o_ref):
    h = _rms(mem_ref[0], g_ref[0]).astype(BF16)
    o_ref[0, 0] = _dot(h, w_ref[0]).astype(BF16)


def _memkv(mem, norm_mem, w_xkv_bf):
    depth = w_xkv_bf.shape[0]
    bsz = mem.shape[0]
    return pl.pallas_call(
        _memkv_kernel,
        grid=(depth, bsz),
        in_specs=[
            pl.BlockSpec((1, MEM_LEN, D_MODEL), lambda l, b: (b, 0, 0)),
            pl.BlockSpec((1, 1, D_MODEL), lambda l, b: (l, 0, 0)),
            pl.BlockSpec((1, D_MODEL, 2 * XA_WIDTH), lambda l, b: (l, 0, 0)),
        ],
        out_specs=pl.BlockSpec((1, 1, MEM_LEN, 2 * XA_WIDTH), lambda l, b: (l, b, 0, 0)),
        out_shape=jax.ShapeDtypeStruct((depth, bsz, MEM_LEN, 2 * XA_WIDTH), BF16),
        compiler_params=_params("arbitrary", "arbitrary"),
        name="memkv",
    )(mem, norm_mem.reshape(depth, 1, D_MODEL), w_xkv_bf)


def _inproj_ssm_kernel(x_ref, g_ref, wa_ref, cos_ref, sina_ref, sinb_ref, bw_ref, lr_ref, li_ref,
                       cw_ref, d_ref, wglu_ref, qkv_ref, yssm_ref, u_tb, bux, st, y_tb, *, ts, bsz):
    @pl.when(pl.program_id(0) == 0)
    def _():
        st[...] = jnp.zeros_like(st)

    gain = g_ref[...]
    for b in range(bsz):
        h = _rms(x_ref[b], gain).astype(BF16)
        p = _dot(h, wa_ref[...])
        for w in range(SSM_WIN):
            u_tb[w, pl.ds(b, ts, stride=bsz), :] = p[:, LANES * w:LANES * (w + 1)]
        o = SSM_WIDTH
        qkv_ref[b, :, 0:SB_WIDTH] = (p[:, o:o + SB_WIDTH] * SB_HEAD_DIM ** -0.5).astype(BF16)
        qkv_ref[b, :, SB_WIDTH:3 * SB_WIDTH] = p[:, o + SB_WIDTH:o + 3 * SB_WIDTH].astype(BF16)
        cosf, sina, sinb = cos_ref[b], sina_ref[b], sinb_ref[b]
        src = o + 3 * SB_WIDTH
        dst = 3 * SB_WIDTH
        for scale in (DA_QK_DIM ** -0.5, 1.0):
            for hh in range(DA_HEADS):
                t = p[:, src + LANES * hh:src + LANES * (hh + 1)]
                r = (t * cosf + pltpu.roll(t, ROT_DIM // 2, 1) * sina
                     + pltpu.roll(t, LANES - ROT_DIM // 2, 1) * sinb)
                qkv_ref[b, :, dst + LANES * hh:dst + LANES * (hh + 1)] = (r * scale).astype(BF16)
            src += DA_QK_WIDTH
            dst += DA_QK_WIDTH
        qkv_ref[b, :, dst:dst + DA_WIDTH] = p[:, src:src + DA_WIDTH].astype(BF16)

    wst = 2 * SSM_WIN_STATE
    for w in range(SSM_WIN):
        bux[:, wst * w:wst * (w + 1)] = _dot(u_tb[w].astype(BF16), bw_ref[w])

    for w in range(SSM_WIN):
        c0 = wst * w
        c1 = c0 + SSM_WIN_STATE
        c2 = c0 + wst
        lr = lr_ref[:, SSM_WIN_STATE * w:SSM_WIN_STATE * (w + 1)]
        li = li_ref[:, SSM_WIN_STATE * w:SSM_WIN_STATE * (w + 1)]

        def body(t, carry, c0=c0, c1=c1, c2=c2, lr=lr, li=li):
            xr, xi = carry
            r = pl.multiple_of(t * bsz, bsz)
            nxr = lr * xr - li * xi + bux[pl.ds(r, bsz), c0:c1]
            nxi = lr * xi + li * xr + bux[pl.ds(r, bsz), c1:c2]
            bux[pl.ds(r, bsz), c0:c1] = nxr
            bux[pl.ds(r, bsz), c1:c2] = nxi
            return nxr, nxi

        xr, xi = lax.fori_loop(0, ts, body, (st[:, c0:c1], st[:, c1:c2]), unroll=2)
        st[:, c0:c1] = xr
        st[:, c1:c2] = xi

    ys = []
    for w in range(SSM_WIN):
        xw = bux[:, wst * w:wst * (w + 1)].astype(BF16)
        yw = _dot(xw, cw_ref[w])
        yw = yw + d_ref[:, LANES * w:LANES * (w + 1)] * u_tb[w]
        ys.append(jax.nn.gelu(yw).astype(BF16))
    glu = _dot(jnp.concatenate(ys, axis=1), wglu_ref[...])
    for w in range(SSM_WIN):
        sl = slice(LANES * w, LANES * (w + 1))
        y_tb[w] = glu[:, sl] * jax.nn.sigmoid(glu[:, SSM_WIDTH + LANES * w:SSM_WIDTH + LANES * (w + 1)])
    for b in range(bsz):
        for w in range(SSM_WIN):
            yssm_ref[b, :, LANES * w:LANES * (w + 1)] = y_tb[w, pl.ds(b, ts, stride=bsz), :].astype(BF16)


def _inproj_ssm(x, gain, wa, rope, ssm, d_skip, w_glu, *, ts):
    bsz, seq, _ = x.shape
    assert bsz == SUBLANES and seq % ts == 0
    cosf, sina, sinb = rope
    bw, lr, li, cw = ssm
    m = ts * bsz
    tile = lambda width: pl.BlockSpec((bsz, ts, width), lambda i: (0, i, 0))
    return pl.pallas_call(
        functools.partial(_inproj_ssm_kernel, ts=ts, bsz=bsz),
        grid=(seq // ts,),
        in_specs=[
            tile(D_MODEL),
            _const_spec((1, D_MODEL)),
            _const_spec((D_MODEL, N_PROJ_A)),
            tile(LANES), tile(LANES), tile(LANES),
            _const_spec(bw.shape), _const_spec(lr.shape), _const_spec(li.shape), _const_spec(cw.shape),
            _const_spec((1, SSM_WIDTH)),
            _const_spec((SSM_WIDTH, 2 * SSM_WIDTH)),
        ],
        out_specs=[tile(N_QKV), tile(SSM_WIDTH)],
        out_shape=[jax.ShapeDtypeStruct((bsz, seq, N_QKV), BF16),
                   jax.ShapeDtypeStruct((bsz, seq, SSM_WIDTH), BF16)],
        scratch_shapes=[
            pltpu.VMEM((SSM_WIN, m, LANES), F32),
            pltpu.VMEM((m, 2 * SSM_WIN * SSM_WIN_STATE), F32),
            pltpu.VMEM((bsz, 2 * SSM_WIN * SSM_WIN_STATE), F32),
            pltpu.VMEM((SSM_WIN, m, LANES), F32),
        ],
        compiler_params=_params("arbitrary"),
        name="inproj_ssm",
    )(x, gain.reshape(1, D_MODEL), wa, cosf, sina, sinb, bw, lr, li, cw,
      d_skip.reshape(1, SSM_WIDTH), w_glu)


def _sb_kernel(q_ref, k_ref, v_ref, o_ref, *, tq):
    i = pl.program_id(2)
    q2 = q_ref[0]
    lane = lax.broadcasted_iota(jnp.int32, (tq, LANES), 1)
    row = lax.broadcasted_iota(jnp.int32, (tq, tq), 0)
    col = lax.broadcasted_iota(jnp.int32, (tq, tq), 1)
    strict = col < row
    tri = (row > col).astype(BF16)

    zero = jnp.zeros_like(q2)
    qs = (jnp.where(lane < SB_HEAD_DIM, q2, zero), jnp.where(lane >= SB_HEAD_DIM, q2, zero))

    def block(j, masked, runs):
        ks = pl.multiple_of(j * tq, tq)
        kb = k_ref[0, pl.ds(ks, tq), :]
        vb = v_ref[0, pl.ds(ks, tq), :]
        pvs, new_runs = [], []
        for a in range(2):
            z = _dot_nt(qs[a], kb)
            l1m = -(jnp.maximum(z, 0.0) + jnp.log(1.0 + jnp.exp(-jnp.abs(z))))
            if masked:
                l1m = jnp.where(strict, l1m, 0.0)
            hi = l1m.astype(BF16)
            lo = (l1m - hi.astype(F32)).astype(BF16)
            later = _dot(hi, tri) + _dot(lo, tri) + runs[a]
            w = jnp.exp(z + l1m + later)
            if masked:
                w = jnp.where(strict, w, 0.0)
            pvs.append(_dot(w.astype(BF16), vb))
            new_runs.append(runs[a] + jnp.sum(l1m, axis=1, keepdims=True))
        return pvs, new_runs

    zrun = jnp.zeros((tq, 1), F32)
    (acc0, acc1), (run0, run1) = block(i, True, (zrun, zrun))

    def cond(c):
        j, r0, r1, _, _ = c
        return jnp.logical_and(j >= 0, jnp.max(jnp.maximum(r0, r1)) > SB_LOG_CUTOFF)

    def body(c):
        j, r0, r1, a0, a1 = c
        (p0, p1), (r0, r1) = block(j, False, (r0, r1))
        return j - 1, r0, r1, a0 + p0, a1 + p1

    _, _, _, acc0, acc1 = lax.while_loop(cond, body, (i - 1, run0, run1, acc0, acc1))
    o_ref[0] = jnp.where(lane < SB_HEAD_DIM, acc0, acc1).astype(BF16)


def _sb_attention(qkv, *, tq):
    bsz, seq, _ = qkv.shape
    npair = SB_WIDTH // LANES
    return pl.pallas_call(
        functools.partial(_sb_kernel, tq=tq),
        grid=(bsz, npair, seq // tq),
        in_specs=[
            pl.BlockSpec((1, tq, LANES), lambda b, p, i: (b, i, SBQ_BLK + p)),
            pl.BlockSpec((1, seq, LANES), lambda b, p, i: (b, 0, SBK_BLK + p)),
            pl.BlockSpec((1, seq, LANES), lambda b, p, i: (b, 0, SBV_BLK + p)),
        ],
        out_specs=pl.BlockSpec((1, tq, LANES), lambda b, p, i: (b, i, p)),
        out_shape=jax.ShapeDtypeStruct((bsz, seq, SB_WIDTH), BF16),
        compiler_params=_params("parallel", "parallel", "arbitrary"),
        name="sb_attn",
    )(qkv, qkv, qkv)


def _da_kernel(q_ref, k_ref, v_ref, lam_ref, g_ref, o_ref, *, tq, tk, lam_init):
    i = pl.program_id(2)
    q = q_ref[0]
    lane = lax.broadcasted_iota(jnp.int32, (tq, LANES), 1)
    zero = jnp.zeros_like(q)
    qs = (jnp.where(lane < DA_QK_DIM, q, zero), jnp.where(lane >= DA_QK_DIM, q, zero))
    nfull = (i * tq) // tk

    def step(j, masked, carry):
        ks = pl.multiple_of(j * tk, tk)
        kb = k_ref[0, pl.ds(ks, tk), :]
        vb = v_ref[0, pl.ds(ks, tk), :]
        if masked:
            row = lax.broadcasted_iota(jnp.int32, (tq, tk), 0) + i * tq
            col = lax.broadcasted_iota(jnp.int32, (tq, tk), 1) + j * tk
            causal = col <= row
        out = []
        for c in range(2):
            m, l, acc = carry[c]
            s = _dot_nt(qs[c], kb)
            if masked:
                s = jnp.where(causal, s, MASK_VALUE)
            mn = jnp.maximum(m, jnp.max(s, axis=1, keepdims=True))
            alpha = jnp.exp(m - mn)
            p = jnp.exp(s - mn)
            l = alpha * l + jnp.sum(p, axis=1, keepdims=True)
            acc = alpha * acc + _dot(p.astype(BF16), vb)
            out.append((mn, l, acc))
        return tuple(out)

    init = tuple((jnp.full((tq, 1), MASK_VALUE, F32), jnp.zeros((tq, 1), F32), jnp.zeros((tq, LANES), F32))
                 for _ in range(2))
    carry = lax.fori_loop(0, nfull, lambda j, c: step(j, False, c), init)
    (_, l1, a1), (_, l2, a2) = step(nfull, True, carry)

    lv = lam_ref[...]
    lam = (jnp.exp(jnp.sum(lv[0:1] * lv[1:2], axis=1, keepdims=True))
           - jnp.exp(jnp.sum(lv[2:3] * lv[3:4], axis=1, keepdims=True)) + lam_init)
    o = a1 / l1 - lam * (a2 / l2)
    o_ref[0] = (_rms(o, g_ref[...]) * (1.0 - lam_init)).astype(BF16)


def _da_attention(qkv, diff_lambda, diff_subln, lam_init, *, tq, tk):
    bsz, seq, _ = qkv.shape
    assert tk % tq == 0 and seq % tk == 0
    return pl.pallas_call(
        functools.partial(_da_kernel, tq=tq, tk=tk, lam_init=lam_init),
        grid=(bsz, DA_HEADS, seq // tq),
        in_specs=[
            pl.BlockSpec((1, tq, LANES), lambda b, h, i: (b, i, DAQ_BLK + h)),
            pl.BlockSpec((1, seq, LANES), lambda b, h, i: (b, 0, DAK_BLK + h)),
            pl.BlockSpec((1, seq, LANES), lambda b, h, i: (b, 0, DAV_BLK + h)),
            pl.BlockSpec((4, DA_QK_DIM), lambda b, h, i: (0, 0)),
            pl.BlockSpec((1, DA_V_DIM), lambda b, h, i: (0, 0)),
        ],
        out_specs=pl.BlockSpec((1, tq, LANES), lambda b, h, i: (b, i, h)),
        out_shape=jax.ShapeDtypeStruct((bsz, seq, DA_WIDTH), BF16),
        compiler_params=_params("parallel", "parallel", "arbitrary"),
        name="da_attn",
    )(qkv, qkv, qkv, diff_lambda, diff_subln.reshape(1, DA_V_DIM))


def _merge_cross_kernel(x_ref, yssm_ref, ysb_ref, yda_ref, gmix_ref, wg_ref, wbr_ref, wout_ref,
                        gcross_ref, wxq_ref, kv_ref, wxo_ref, o_ref):
    x = x_ref[0]
    h = _rms(x, gmix_ref[...]).astype(BF16)
    merged = None
    for n, y_ref in enumerate((yssm_ref, ysb_ref, yda_ref)):
        gate = jax.nn.sigmoid(_dot(h, wg_ref[:, D_MODEL * n:D_MODEL * (n + 1)]))
        term = gate * _dot(y_ref[0], wbr_ref[n])
        merged = term if merged is None else merged + term
    x1 = x + _dot(merged.astype(BF16), wout_ref[...])

    hx = _rms(x1, gcross_ref[...]).astype(BF16)
    xq = _dot(hx, wxq_ref[...]).astype(BF16)
    kv = kv_ref[0]
    heads = []
    for hh in range(XA_HEADS):
        sl = slice(XA_HEAD_DIM * hh, XA_HEAD_DIM * (hh + 1))
        s = _dot_nt(xq[:, sl], kv[:, sl]) * XA_HEAD_DIM ** -0.5
        p = jnp.exp(s - jnp.max(s, axis=1, keepdims=True))
        vh = kv[:, XA_WIDTH + XA_HEAD_DIM * hh:XA_WIDTH + XA_HEAD_DIM * (hh + 1)]
        heads.append((_dot(p.astype(BF16), vh) / jnp.sum(p, axis=1, keepdims=True)).astype(BF16))
    o_ref[0] = x1 + _dot(jnp.concatenate(heads, axis=1), wxo_ref[...])


def _merge_cross(x, yssm, ysb, yda, gmix, wg, wbr, wout, gcross, wxq, kv, wxo, *, tm):
    bsz, seq, _ = x.shape
    tile = lambda width: pl.BlockSpec((1, tm, width), lambda b, i: (b, i, 0))
    return pl.pallas_call(
        _merge_cross_kernel,
        grid=(bsz, seq // tm),
        in_specs=[
            tile(D_MODEL), tile(BRANCH_WIDTH), tile(BRANCH_WIDTH), tile(BRANCH_WIDTH),
            _const_spec((1, D_MODEL)),
            _const_spec((D_MODEL, N_BRANCH * D_MODEL)),
            _const_spec((N_BRANCH, BRANCH_WIDTH, D_MODEL)),
            _const_spec((D_MODEL, D_MODEL)),
            _const_spec((1, D_MODEL)),
            _const_spec((D_MODEL, XA_WIDTH)),
            pl.BlockSpec((1, MEM_LEN, 2 * XA_WIDTH), lambda b, i: (b, 0, 0)),
            _const_spec((XA_WIDTH, D_MODEL)),
        ],
        out_specs=tile(D_MODEL),
        out_shape=jax.ShapeDtypeStruct((bsz, seq, D_MODEL), F32),
        compiler_params=_params("parallel", "arbitrary"),
        name="merge_cross",
    )(x, yssm, ysb, yda, gmix.reshape(1, D_MODEL), wg, wbr, wout, gcross.reshape(1, D_MODEL), wxq, kv, wxo)


def _mlp_kernel(x_ref, g_ref, wup_ref, wdown_ref, gfin_ref, o_ref, *, final, chunk):
    x = x_ref[...]
    h = _rms(x, g_ref[...]).astype(BF16)
    acc = x
    for c in range(MLP_HIDDEN // chunk):
        up = _dot(h, wup_ref[:, chunk * c:chunk * (c + 1)])
        act = jnp.square(jnp.maximum(up, 0.0)).astype(BF16)
        acc = acc + _dot(act, wdown_ref[chunk * c:chunk * (c + 1), :])
    if final:
        acc = _rms(acc, gfin_ref[...])
    o_ref[...] = acc


def _mlp(x2d, gain, wup, wdown, gfin, *, final, tm):
    n = x2d.shape[0]
    tile = pl.BlockSpec((tm, D_MODEL), lambda i: (i, 0))
    return pl.pallas_call(
        functools.partial(_mlp_kernel, final=final, chunk=1024),
        grid=(n // tm,),
        in_specs=[
            tile,
            _const_spec((1, D_MODEL)),
            _const_spec((D_MODEL, MLP_HIDDEN)),
            _const_spec((MLP_HIDDEN, D_MODEL)),
            _const_spec((1, D_MODEL)),
        ],
        out_specs=tile,
        out_shape=jax.ShapeDtypeStruct((n, D_MODEL), F32),
        compiler_params=_params("parallel"),
        name="mlp",
    )(x2d, gain.reshape(1, D_MODEL), wup, wdown, gfin.reshape(1, D_MODEL))


def _ssm_tables(lam_re, lam_im, log_dt, b_re, b_im, c_re, c_im, bsz):
    dt = jnp.exp(log_dt)[:, None]
    mag = jnp.exp(lam_re * dt)
    lbr = mag * jnp.cos(lam_im * dt)
    lbi = mag * jnp.sin(lam_im * dt)
    den = lam_re * lam_re + lam_im * lam_im
    fr = ((lbr - 1.0) * lam_re + lbi * lam_im) / den
    fi = (lbi * lam_re - (lbr - 1.0) * lam_im) / den
    bbr = fr[..., None] * b_re - fi[..., None] * b_im
    bbi = fr[..., None] * b_im + fi[..., None] * b_re
    gl = SSM_GROUPS // SSM_WIN
    eye = jnp.eye(gl, dtype=F32)
    shp = (SSM_WIN, gl, SSM_STATE, SSM_GROUP)
    to_b = lambda t: jnp.einsum("wgpc,gh->wgchp", t.reshape(shp), eye)
    bw = jnp.stack([to_b(bbr), to_b(bbi)], axis=3).reshape(SSM_WIN, LANES, 2 * SSM_WIN_STATE)
    shc = (SSM_WIN, gl, SSM_GROUP, SSM_STATE)
    to_c = lambda t: jnp.einsum("wgcp,gh->wgphc", t.reshape(shc), eye)
    cw = jnp.stack([to_c(c_re), -to_c(c_im)], axis=1).reshape(SSM_WIN, 2 * SSM_WIN_STATE, LANES)
    lr = jnp.broadcast_to(lbr.reshape(1, -1), (bsz, SSM_GROUPS * SSM_STATE))
    li = jnp.broadcast_to(lbi.reshape(1, -1), (bsz, SSM_GROUPS * SSM_STATE))
    return bw.astype(BF16), lr, li, cw.astype(BF16)


def _rope_tables(positions):
    half = ROT_DIM // 2
    inv_freq = ROPE_THETA ** (-jnp.arange(0, ROT_DIM, 2, dtype=F32) / ROT_DIM)
    ang = positions.astype(F32)[..., None] * inv_freq
    cos, sin = jnp.cos(ang), jnp.sin(ang)
    lead = ang.shape[:-1]
    ones = jnp.ones(lead + (DA_QK_DIM - ROT_DIM,), F32)
    zeros = jnp.zeros(lead + (DA_QK_DIM - ROT_DIM,), F32)
    zh = jnp.zeros(lead + (half,), F32)
    rep = lambda t: jnp.concatenate([t, t], axis=-1)
    cosf = rep(jnp.concatenate([cos, cos, ones], axis=-1))
    sina = rep(jnp.concatenate([zh, sin, zeros], axis=-1))
    sinb = rep(jnp.concatenate([-sin, zh, zeros], axis=-1))
    return cosf, sina, sinb


def kernel(x, mem, positions, norm_mix, w_in, ssm_lam_re, ssm_lam_im, ssm_log_dt, ssm_b_re, ssm_b_im, ssm_c_re, ssm_c_im, ssm_d, ssm_w_glu, diff_lambda, diff_subln, w_branch, w_out, norm_cross, norm_mem, w_xq, w_xkv, w_xo, norm_mlp, w_up, w_down, norm_final):
    bsz, seq, _ = x.shape
    depth = w_in.shape[0]
    ts = min(64, seq)
    sb_tq = min(256, seq)
    da_tq = min(256, seq)
    da_tk = min(512, seq)
    tm = min(512, seq)

    rope = _rope_tables(positions)
    kv_all = _memkv(mem, norm_mem, w_xkv.astype(BF16))

    for l in range(depth):
        wa = w_in[l, :, :N_PROJ_A].astype(BF16)
        wg = w_in[l, :, N_PROJ_A:].astype(BF16)
        ssm = _ssm_tables(ssm_lam_re[l], ssm_lam_im[l], ssm_log_dt[l], ssm_b_re[l], ssm_b_im[l],
                          ssm_c_re[l], ssm_c_im[l], bsz)
        qkv, yssm = _inproj_ssm(x, norm_mix[l], wa, rope, ssm, ssm_d[l], ssm_w_glu[l].astype(BF16), ts=ts)
        ysb = _sb_attention(qkv, tq=sb_tq)
        lam_init = 0.8 - 0.6 * math.exp(-0.3 * l)
        yda = _da_attention(qkv, diff_lambda[l], diff_subln[l], lam_init, tq=da_tq, tk=da_tk)
        x = _merge_cross(x, yssm, ysb, yda, norm_mix[l], wg, w_branch[l].astype(BF16), w_out[l].astype(BF16),
                         norm_cross[l], w_xq[l].astype(BF16), kv_all[l], w_xo[l].astype(BF16), tm=tm)
        x = _mlp(x.reshape(bsz * seq, D_MODEL), norm_mlp[l], w_up[l].astype(BF16), w_down[l].astype(BF16),
                 norm_final, final=(l == depth - 1), tm=tm).reshape(bsz, seq, D_MODEL)
    return x
```

```python
import functools
import math

import jax
import jax.numpy as jnp
from jax import lax
from jax.experimental import pallas as pl
from jax.experimental.pallas import tpu as pltpu

F32 = jnp.float32
BF16 = jnp.bfloat16

D_MODEL = 1024
MEM_LEN = 256
RMS_EPS = 1e-6
SSM_WIDTH = 512
SSM_GROUP = 16
SSM_GROUPS = 32
SSM_STATE = 64
SB_HEADS = 8
SB_HEAD_DIM = 64
SB_WIDTH = 512
DA_HEADS = 4
DA_QK_DIM = 64
DA_V_DIM = 128
DA_QK_WIDTH = 512
DA_WIDTH = 512
ROT_DIM = 16
ROPE_THETA = 500000.0
N_BRANCH = 3
BRANCH_WIDTH = 512
XA_HEADS = 4
XA_HEAD_DIM = 128
XA_WIDTH = 512
MLP_HIDDEN = 4096

LANES = 128
SUBLANES = 8
VMEM_LIMIT_BYTES = 58 * 1024 * 1024

N_PROJ_A = SSM_WIDTH + 3 * SB_WIDTH + 2 * DA_QK_WIDTH + DA_WIDTH
N_QKV = N_PROJ_A - SSM_WIDTH
SBQ_BLK, SBK_BLK, SBV_BLK = 0, 4, 8
DAQ_BLK, DAK_BLK, DAV_BLK = 12, 16, 20

SSM_WIN = SSM_WIDTH // LANES
SSM_WIN_STATE = (SSM_GROUPS // SSM_WIN) * SSM_STATE

LOG2_E = 1.4426950408889634
SB_LOG_CUTOFF = -150.0
MASK_VALUE = -1e30


def _rms(x, gain):
    ms = jnp.mean(x * x, axis=-1, keepdims=True)
    return x * lax.rsqrt(ms + RMS_EPS) * gain


def _dot(a, b):
    return jnp.dot(a, b, preferred_element_type=F32)


def _dot_nt(a, b):
    return lax.dot_general(a, b, (((1,), (1,)), ((), ())), preferred_element_type=F32)


def _params(*sem):
    return pltpu.CompilerParams(dimension_semantics=sem, vmem_limit_bytes=VMEM_LIMIT_BYTES)


def _const_spec(shape):
    nd = len(shape)
    return pl.BlockSpec(shape, lambda *_: (0,) * nd, pipeline_mode=pl.Buffered(1))


def _memkv_kernel(mem_ref, g_ref, w_ref, o_ref):
    h = _rms(mem_ref[0], g_ref[0]).astype(BF16)
    o_ref[0, 0] = _dot(h, w_ref[0]).astype(BF16)


def _memkv(mem, norm_mem, w_xkv_bf):
    depth = w_xkv_bf.shape[0]
    bsz = mem.shape[0]
    return pl.pallas_call(
        _memkv_kernel,
        grid=(depth, bsz),
        in_specs=[
            pl.BlockSpec((1, MEM_LEN, D_MODEL), lambda l, b: (b, 0, 0)),
            pl.BlockSpec((1, 1, D_MODEL), lambda l, b: (l, 0, 0)),
            pl.BlockSpec((1, D_MODEL, 2 * XA_WIDTH), lambda l, b: (l, 0, 0)),
        ],
        out_specs=pl.BlockSpec((1, 1, MEM_LEN, 2 * XA_WIDTH), lambda l, b: (l, b, 0, 0)),
        out_shape=jax.ShapeDtypeStruct((depth, bsz, MEM_LEN, 2 * XA_WIDTH), BF16),
        compiler_params=_params("arbitrary", "arbitrary"),
        name="memkv",
    )(mem, norm_mem.reshape(depth, 1, D_MODEL), w_xkv_bf)


def _inproj_ssm_kernel(x_ref, g_ref, wa_ref, cos_ref, sina_ref, sinb_ref, bw_ref, lr_ref, li_ref,
                       cw_ref, d_ref, wglu_ref, qkv_ref, yssm_ref, u_tb, bux, st, y_tb, *, ts, bsz):
    @pl.when(pl.program_id(0) == 0)
    def _():
        st[...] = jnp.zeros_like(st)

    m = ts * bsz
    h = _rms(x_ref[...].reshape(m, D_MODEL), g_ref[...]).astype(BF16)

    def proj(c0, width=SSM_WIDTH):
        return _dot(h, wa_ref[:, c0:c0 + width])

    def put(dst, val):
        qkv_ref[:, :, dst:dst + val.shape[1]] = val.astype(BF16).reshape(bsz, ts, val.shape[1])

    u = proj(0)
    for b in range(bsz):
        for w in range(SSM_WIN):
            u_tb[w, pl.ds(b, ts, stride=bsz), :] = u[ts * b:ts * (b + 1), LANES * w:LANES * (w + 1)]
    o = SSM_WIDTH
    put(0, proj(o) * (SB_HEAD_DIM ** -0.5 * LOG2_E))
    put(SB_WIDTH, proj(o + SB_WIDTH))
    put(2 * SB_WIDTH, proj(o + 2 * SB_WIDTH))
    cosf = cos_ref[...].reshape(m, LANES)
    sina = sina_ref[...].reshape(m, LANES)
    sinb = sinb_ref[...].reshape(m, LANES)
    src = o + 3 * SB_WIDTH
    dst = 3 * SB_WIDTH
    for scale in (DA_QK_DIM ** -0.5 * LOG2_E, 1.0):
        t4 = proj(src)
        for hh in range(DA_HEADS):
            t = t4[:, LANES * hh:LANES * (hh + 1)]
            r = (t * cosf + pltpu.roll(t, ROT_DIM // 2, 1) * sina
                 + pltpu.roll(t, LANES - ROT_DIM // 2, 1) * sinb)
            put(dst + LANES * hh, r * scale)
        src += DA_QK_WIDTH
        dst += DA_QK_WIDTH
    put(dst, proj(src))

    wst = 2 * SSM_WIN_STATE
    for w in range(SSM_WIN):
        bux[:, wst * w:wst * (w + 1)] = _dot(u_tb[w].astype(BF16), bw_ref[w])

    nhalf = 2 * SSM_WIN
    hs = SSM_WIN_STATE

    def body(t, carry):
        r = pl.multiple_of(t * bsz, bsz)
        new = []
        for w in range(SSM_WIN):
            xr, xi = carry[2 * w], carry[2 * w + 1]
            lr = lr_ref[:, hs * w:hs * (w + 1)]
            li = li_ref[:, hs * w:hs * (w + 1)]
            cr = slice(hs * 2 * w, hs * (2 * w + 1))
            ci = slice(hs * (2 * w + 1), hs * (2 * w + 2))
            nxr = lr * xr - li * xi + bux[pl.ds(r, bsz), cr]
            nxi = lr * xi + li * xr + bux[pl.ds(r, bsz), ci]
            bux[pl.ds(r, bsz), cr] = nxr
            bux[pl.ds(r, bsz), ci] = nxi
            new += [nxr, nxi]
        return tuple(new)

    fin = lax.fori_loop(0, ts, body, tuple(st[:, hs * k:hs * (k + 1)] for k in range(nhalf)), unroll=2)
    for k in range(nhalf):
        st[:, hs * k:hs * (k + 1)] = fin[k]

    ys = []
    for w in range(SSM_WIN):
        xw = bux[:, wst * w:wst * (w + 1)].astype(BF16)
        yw = _dot(xw, cw_ref[w])
        yw = yw + d_ref[:, LANES * w:LANES * (w + 1)] * u_tb[w]
        ys.append(jax.nn.gelu(yw).astype(BF16))
    glu = _dot(jnp.concatenate(ys, axis=1), wglu_ref[...])
    for w in range(SSM_WIN):
        sl = slice(LANES * w, LANES * (w + 1))
        y_tb[w] = glu[:, sl] * jax.nn.sigmoid(glu[:, SSM_WIDTH + LANES * w:SSM_WIDTH + LANES * (w + 1)])
    for b in range(bsz):
        for w in range(SSM_WIN):
            yssm_ref[b, :, LANES * w:LANES * (w + 1)] = y_tb[w, pl.ds(b, ts, stride=bsz), :].astype(BF16)


def _inproj_ssm(x, gain, wa, rope, ssm, d_skip, w_glu, *, ts):
    bsz, seq, _ = x.shape
    assert bsz == SUBLANES and seq % ts == 0
    cosf, sina, sinb = rope
    bw, lr, li, cw = ssm
    m = ts * bsz
    tile = lambda width: pl.BlockSpec((bsz, ts, width), lambda i: (0, i, 0))
    return pl.pallas_call(
        functools.partial(_inproj_ssm_kernel, ts=ts, bsz=bsz),
        grid=(seq // ts,),
        in_specs=[
            tile(D_MODEL),
            _const_spec((1, D_MODEL)),
            _const_spec((D_MODEL, N_PROJ_A)),
            tile(LANES), tile(LANES), tile(LANES),
            _const_spec(bw.shape), _const_spec(lr.shape), _const_spec(li.shape), _const_spec(cw.shape),
            _const_spec((1, SSM_WIDTH)),
            _const_spec((SSM_WIDTH, 2 * SSM_WIDTH)),
        ],
        out_specs=[tile(N_QKV), tile(SSM_WIDTH)],
        out_shape=[jax.ShapeDtypeStruct((bsz, seq, N_QKV), BF16),
                   jax.ShapeDtypeStruct((bsz, seq, SSM_WIDTH), BF16)],
        scratch_shapes=[
            pltpu.VMEM((SSM_WIN, m, LANES), F32),
            pltpu.VMEM((m, 2 * SSM_WIN * SSM_WIN_STATE), F32),
            pltpu.VMEM((bsz, 2 * SSM_WIN * SSM_WIN_STATE), F32),
            pltpu.VMEM((SSM_WIN, m, LANES), F32),
        ],
        compiler_params=_params("arbitrary"),
        name="inproj_ssm",
    )(x, gain.reshape(1, D_MODEL), wa, cosf, sina, sinb, bw, lr, li, cw,
      d_skip.reshape(1, SSM_WIDTH), w_glu)


def _sb_kernel(q_ref, k_ref, v_ref, o_ref, *, tq):
    i = pl.program_id(2)
    q2 = q_ref[0]
    lane = lax.broadcasted_iota(jnp.int32, (tq, LANES), 1)
    row = lax.broadcasted_iota(jnp.int32, (tq, tq), 0)
    col = lax.broadcasted_iota(jnp.int32, (tq, tq), 1)
    strict = col < row
    tri = (row > col).astype(BF16)

    zero = jnp.zeros_like(q2)
    qs = (jnp.where(lane < SB_HEAD_DIM, q2, zero), jnp.where(lane >= SB_HEAD_DIM, q2, zero))

    def block(j, masked, runs):
        ks = pl.multiple_of(j * tq, tq)
        kb = k_ref[0, pl.ds(ks, tq), :]
        vb = v_ref[0, pl.ds(ks, tq), :]
        pvs, new_runs = [], []
        for a in range(2):
            z = _dot_nt(qs[a], kb)
            l1m = -(jnp.maximum(z, 0.0) + jnp.log2(1.0 + jnp.exp2(-jnp.abs(z))))
            if masked:
                l1m = jnp.where(strict, l1m, 0.0)
            hi = l1m.astype(BF16)
            lo = (l1m - hi.astype(F32)).astype(BF16)
            later = _dot(hi, tri) + _dot(lo, tri) + runs[a]
            w = jnp.exp2(z + l1m + later)
            if masked:
                w = jnp.where(strict, w, 0.0)
            pvs.append(_dot(w.astype(BF16), vb))
            new_runs.append(runs[a] + jnp.sum(l1m, axis=1, keepdims=True))
        return pvs, new_runs

    zrun = jnp.zeros((tq, 1), F32)
    (acc0, acc1), (run0, run1) = block(i, True, (zrun, zrun))

    def cond(c):
        j, r0, r1, _, _ = c
        return jnp.logical_and(j >= 0, jnp.max(jnp.maximum(r0, r1)) > SB_LOG_CUTOFF)

    def body(c):
        j, r0, r1, a0, a1 = c
        (p0, p1), (r0, r1) = block(j, False, (r0, r1))
        return j - 1, r0, r1, a0 + p0, a1 + p1

    _, _, _, acc0, acc1 = lax.while_loop(cond, body, (i - 1, run0, run1, acc0, acc1))
    o_ref[0] = jnp.where(lane < SB_HEAD_DIM, acc0, acc1).astype(BF16)


def _sb_attention(qkv, *, tq):
    bsz, seq, _ = qkv.shape
    npair = SB_WIDTH // LANES
    return pl.pallas_call(
        functools.partial(_sb_kernel, tq=tq),
        grid=(bsz, npair, seq // tq),
        in_specs=[
            pl.BlockSpec((1, tq, LANES), lambda b, p, i: (b, i, SBQ_BLK + p)),
            pl.BlockSpec((1, seq, LANES), lambda b, p, i: (b, 0, SBK_BLK + p)),
            pl.BlockSpec((1, seq, LANES), lambda b, p, i: (b, 0, SBV_BLK + p)),
        ],
        out_specs=pl.BlockSpec((1, tq, LANES), lambda b, p, i: (b, i, p)),
        out_shape=jax.ShapeDtypeStruct((bsz, seq, SB_WIDTH), BF16),
        compiler_params=_params("parallel", "parallel", "arbitrary"),
        name="sb_attn",
    )(qkv, qkv, qkv)


def _da_kernel(q_ref, k_ref, v_ref, lam_ref, g_ref, o_ref, vt, sa, sb, acc, *, tq, th, lam_init):
    i = pl.program_id(2)

    @pl.when(i == 0)
    def _():
        for c in range(vt.shape[0]):
            vt[c] = v_ref[0, th * c:th * (c + 1), :].astype(F32).T.astype(BF16)

    q = q_ref[0]
    lane = lax.broadcasted_iota(jnp.int32, (tq, LANES), 1)
    zero = jnp.zeros_like(q)
    qs = (jnp.where(lane < DA_QK_DIM, q, zero), jnp.where(lane >= DA_QK_DIM, q, zero))

    def scores_to(buf, blk):
        ks = pl.multiple_of(blk * th, th)
        kb = k_ref[0, pl.ds(ks, th), :]
        for c in range(2):
            buf[c] = _dot_nt(kb, qs[c])

    def update(buf, blk, stats, masked):
        vtb = vt[blk]
        if masked:
            key = lax.broadcasted_iota(jnp.int32, (th, tq), 0) + blk * th
            qry = lax.broadcasted_iota(jnp.int32, (th, tq), 1) + i * tq
            causal = key <= qry
        out = []
        for c in range(2):
            m, l = stats[c]
            s = buf[c]
            if masked:
                s = jnp.where(causal, s, MASK_VALUE)
            mn = jnp.maximum(m, jnp.max(s, axis=0, keepdims=True))
            alpha = jnp.exp2(m - mn)
            p = jnp.exp2(s - mn)
            l = alpha * l + jnp.sum(p, axis=0, keepdims=True)
            acc[c] = alpha * acc[c] + _dot(vtb, p.astype(BF16))
            out.append((mn, l))
        return tuple(out)

    def pair(j, stats, masked):
        scores_to(sb, 2 * j + 1)
        stats = update(sa, 2 * j, stats, masked)
        if not masked:
            scores_to(sa, 2 * j + 2)
        return update(sb, 2 * j + 1, stats, masked)

    acc[...] = jnp.zeros_like(acc)
    scores_to(sa, 0)
    init = tuple((jnp.full((1, tq), MASK_VALUE, F32), jnp.zeros((1, tq), F32)) for _ in range(2))
    stats = lax.fori_loop(0, i, lambda j, st: pair(j, st, False), init)
    (_, l1), (_, l2) = pair(i, stats, True)

    lv = lam_ref[...]
    lam = (jnp.exp(jnp.sum(lv[0:1] * lv[1:2], axis=1, keepdims=True))
           - jnp.exp(jnp.sum(lv[2:3] * lv[3:4], axis=1, keepdims=True)) + lam_init)
    ot = acc[0] * (1.0 / l1) - acc[1] * (lam / l2)
    ms = jnp.mean(ot * ot, axis=0, keepdims=True)
    ot = ot * (lax.rsqrt(ms + RMS_EPS) * (1.0 - lam_init)) * g_ref[...]
    o_ref[0] = ot.T.astype(BF16)


def _da_attention(qkv, diff_lambda, diff_subln, lam_init, *, tq):
    bsz, seq, _ = qkv.shape
    th = tq // 2
    assert seq % tq == 0
    return pl.pallas_call(
        functools.partial(_da_kernel, tq=tq, th=th, lam_init=lam_init),
        grid=(bsz, DA_HEADS, seq // tq),
        in_specs=[
            pl.BlockSpec((1, tq, LANES), lambda b, h, i: (b, i, DAQ_BLK + h)),
            pl.BlockSpec((1, seq, LANES), lambda b, h, i: (b, 0, DAK_BLK + h)),
            pl.BlockSpec((1, seq, LANES), lambda b, h, i: (b, 0, DAV_BLK + h)),
            pl.BlockSpec((4, DA_QK_DIM), lambda b, h, i: (0, 0)),
            pl.BlockSpec((DA_V_DIM, 1), lambda b, h, i: (0, 0)),
        ],
        out_specs=pl.BlockSpec((1, tq, LANES), lambda b, h, i: (b, i, h)),
        out_shape=jax.ShapeDtypeStruct((bsz, seq, DA_WIDTH), BF16),
        scratch_shapes=[pltpu.VMEM((seq // th, DA_V_DIM, th), BF16),
                        pltpu.VMEM((2, th, tq), F32), pltpu.VMEM((2, th, tq), F32),
                        pltpu.VMEM((2, DA_V_DIM, tq), F32)],
        compiler_params=_params("arbitrary", "arbitrary", "arbitrary"),
        name="da_attn",
    )(qkv, qkv, qkv, diff_lambda, diff_subln.reshape(DA_V_DIM, 1))


def _merge_cross_kernel(x_ref, yssm_ref, ysb_ref, yda_ref, gmix_ref, wg_ref, wbr_ref, wout_ref,
                        gcross_ref, wxq_ref, kv_ref, wxo_ref, o_ref):
    x = x_ref[0]
    h = _rms(x, gmix_ref[...]).astype(BF16)
    merged = None
    for n, y_ref in enumerate((yssm_ref, ysb_ref, yda_ref)):
        gate = jax.nn.sigmoid(_dot(h, wg_ref[:, D_MODEL * n:D_MODEL * (n + 1)]))
        term = gate * _dot(y_ref[0], wbr_ref[n])
        merged = term if merged is None else merged + term
    x1 = x + _dot(merged.astype(BF16), wout_ref[...])

    hx = _rms(x1, gcross_ref[...]).astype(BF16)
    xq = _dot(hx, wxq_ref[...]).astype(BF16)
    kv = kv_ref[0]
    heads = []
    for hh in range(XA_HEADS):
        sl = slice(XA_HEAD_DIM * hh, XA_HEAD_DIM * (hh + 1))
        s = _dot_nt(xq[:, sl], kv[:, sl]) * XA_HEAD_DIM ** -0.5
        p = jnp.exp(s - jnp.max(s, axis=1, keepdims=True))
        vh = kv[:, XA_WIDTH + XA_HEAD_DIM * hh:XA_WIDTH + XA_HEAD_DIM * (hh + 1)]
        heads.append((_dot(p.astype(BF16), vh) / jnp.sum(p, axis=1, keepdims=True)).astype(BF16))
    o_ref[0] = x1 + _dot(jnp.concatenate(heads, axis=1), wxo_ref[...])


def _merge_cross(x, yssm, ysb, yda, gmix, wg, wbr, wout, gcross, wxq, kv, wxo, *, tm):
    bsz, seq, _ = x.shape
    tile = lambda width: pl.BlockSpec((1, tm, width), lambda b, i: (b, i, 0))
    return pl.pallas_call(
        _merge_cross_kernel,
        grid=(bsz, seq // tm),
        in_specs=[
            tile(D_MODEL), tile(BRANCH_WIDTH), tile(BRANCH_WIDTH), tile(BRANCH_WIDTH),
            _const_spec((1, D_MODEL)),
            _const_spec((D_MODEL, N_BRANCH * D_MODEL)),
            _const_spec((N_BRANCH, BRANCH_WIDTH, D_MODEL)),
            _const_spec((D_MODEL, D_MODEL)),
            _const_spec((1, D_MODEL)),
            _const_spec((D_MODEL, XA_WIDTH)),
            pl.BlockSpec((1, MEM_LEN, 2 * XA_WIDTH), lambda b, i: (b, 0, 0)),
            _const_spec((XA_WIDTH, D_MODEL)),
        ],
        out_specs=tile(D_MODEL),
        out_shape=jax.ShapeDtypeStruct((bsz, seq, D_MODEL), F32),
        compiler_params=_params("parallel", "arbitrary"),
        name="merge_cross",
    )(x, yssm, ysb, yda, gmix.reshape(1, D_MODEL), wg, wbr, wout, gcross.reshape(1, D_MODEL), wxq, kv, wxo)


def _mlp_kernel(x_ref, g_ref, wup_ref, wdown_ref, gfin_ref, o_ref, *, final, chunk):
    x = x_ref[...]
    h = _rms(x, g_ref[...]).astype(BF16)
    acc = x
    for c in range(MLP_HIDDEN // chunk):
        up = _dot(h, wup_ref[:, chunk * c:chunk * (c + 1)])
        act = jnp.square(jnp.maximum(up, 0.0)).astype(BF16)
        acc = acc + _dot(act, wdown_ref[chunk * c:chunk * (c + 1), :])
    if final:
        acc = _rms(acc, gfin_ref[...])
    o_ref[...] = acc


def _mlp(x2d, gain, wup, wdown, gfin, *, final, tm):
    n = x2d.shape[0]
    tile = pl.BlockSpec((tm, D_MODEL), lambda i: (i, 0))
    return pl.pallas_call(
        functools.partial(_mlp_kernel, final=final, chunk=1024),
        grid=(n // tm,),
        in_specs=[
            tile,
            _const_spec((1, D_MODEL)),
            _const_spec((D_MODEL, MLP_HIDDEN)),
            _const_spec((MLP_HIDDEN, D_MODEL)),
            _const_spec((1, D_MODEL)),
        ],
        out_specs=tile,
        out_shape=jax.ShapeDtypeStruct((n, D_MODEL), F32),
        compiler_params=_params("parallel"),
        name="mlp",
    )(x2d, gain.reshape(1, D_MODEL), wup, wdown, gfin.reshape(1, D_MODEL))


def _ssm_tables(lam_re, lam_im, log_dt, b_re, b_im, c_re, c_im, bsz):
    dt = jnp.exp(log_dt)[:, None]
    mag = jnp.exp(lam_re * dt)
    lbr = mag * jnp.cos(lam_im * dt)
    lbi = mag * jnp.sin(lam_im * dt)
    den = lam_re * lam_re + lam_im * lam_im
    fr = ((lbr - 1.0) * lam_re + lbi * lam_im) / den
    fi = (lbi * lam_re - (lbr - 1.0) * lam_im) / den
    bbr = fr[..., None] * b_re - fi[..., None] * b_im
    bbi = fr[..., None] * b_im + fi[..., None] * b_re
    gl = SSM_GROUPS // SSM_WIN
    eye = jnp.eye(gl, dtype=F32)
    shp = (SSM_WIN, gl, SSM_STATE, SSM_GROUP)
    to_b = lambda t: jnp.einsum("wgpc,gh->wgchp", t.reshape(shp), eye)
    bw = jnp.stack([to_b(bbr), to_b(bbi)], axis=3).reshape(SSM_WIN, LANES, 2 * SSM_WIN_STATE)
    shc = (SSM_WIN, gl, SSM_GROUP, SSM_STATE)
    to_c = lambda t: jnp.einsum("wgcp,gh->wgphc", t.reshape(shc), eye)
    cw = jnp.stack([to_c(c_re), -to_c(c_im)], axis=1).reshape(SSM_WIN, 2 * SSM_WIN_STATE, LANES)
    lr = jnp.broadcast_to(lbr.reshape(1, -1), (bsz, SSM_GROUPS * SSM_STATE))
    li = jnp.broadcast_to(lbi.reshape(1, -1), (bsz, SSM_GROUPS * SSM_STATE))
    return bw.astype(BF16), lr, li, cw.astype(BF16)


def _rope_tables(positions):
    half = ROT_DIM // 2
    inv_freq = ROPE_THETA ** (-jnp.arange(0, ROT_DIM, 2, dtype=F32) / ROT_DIM)
    ang = positions.astype(F32)[..., None] * inv_freq
    cos, sin = jnp.cos(ang), jnp.sin(ang)
    lead = ang.shape[:-1]
    ones = jnp.ones(lead + (DA_QK_DIM - ROT_DIM,), F32)
    zeros = jnp.zeros(lead + (DA_QK_DIM - ROT_DIM,), F32)
    zh = jnp.zeros(lead + (half,), F32)
    rep = lambda t: jnp.concatenate([t, t], axis=-1)
    cosf = rep(jnp.concatenate([cos, cos, ones], axis=-1))
    sina = rep(jnp.concatenate([zh, sin, zeros], axis=-1))
    sinb = rep(jnp.concatenate([-sin, zh, zeros], axis=-1))
    return cosf, sina, sinb


def kernel(x, mem, positions, norm_mix, w_in, ssm_lam_re, ssm_lam_im, ssm_log_dt, ssm_b_re, ssm_b_im, ssm_c_re, ssm_c_im, ssm_d, ssm_w_glu, diff_lambda, diff_subln, w_branch, w_out, norm_cross, norm_mem, w_xq, w_xkv, w_xo, norm_mlp, w_up, w_down, norm_final):
    bsz, seq, _ = x.shape
    depth = w_in.shape[0]
    ts = min(64, seq)
    sb_tq = min(256, seq)
    da_tq = min(512, seq)
    tm = min(512, seq)

    rope = _rope_tables(positions)
    kv_all = _memkv(mem, norm_mem, w_xkv.astype(BF16))

    for l in range(depth):
        wa = w_in[l, :, :N_PROJ_A].astype(BF16)
        wg = w_in[l, :, N_PROJ_A:].astype(BF16)
        ssm = _ssm_tables(ssm_lam_re[l], ssm_lam_im[l], ssm_log_dt[l], ssm_b_re[l], ssm_b_im[l],
                          ssm_c_re[l], ssm_c_im[l], bsz)
        qkv, yssm = _inproj_ssm(x, norm_mix[l], wa, rope, ssm, ssm_d[l], ssm_w_glu[l].astype(BF16), ts=ts)
        ysb = _sb_attention(qkv, tq=sb_tq)
        lam_init = 0.8 - 0.6 * math.exp(-0.3 * l)
        yda = _da_attention(qkv, diff_lambda[l], diff_subln[l], lam_init, tq=da_tq)
        x = _merge_cross(x, yssm, ysb, yda, norm_mix[l], wg, w_branch[l].astype(BF16), w_out[l].astype(BF16),
                         norm_cross[l], w_xq[l].astype(BF16), kv_all[l], w_xo[l].astype(BF16), tm=tm)
        x = _mlp(x.reshape(bsz * seq, D_MODEL), norm_mlp[l], w_up[l].astype(BF16), w_down[l].astype(BF16),
                 norm_final, final=(l == depth - 1), tm=tm).reshape(bsz, seq, D_MODEL)
    return x
```

```python
import functools
import math

import jax
import jax.numpy as jnp
from jax import lax
from jax.experimental import pallas as pl
from jax.experimental.pallas import tpu as pltpu

F32 = jnp.float32
BF16 = jnp.bfloat16

D_MODEL = 1024
MEM_LEN = 256
RMS_EPS = 1e-6
SSM_WIDTH = 512
SSM_GROUP = 16
SSM_GROUPS = 32
SSM_STATE = 64
SB_HEADS = 8
SB_HEAD_DIM = 64
SB_WIDTH = 512
DA_HEADS = 4
DA_QK_DIM = 64
DA_V_DIM = 128
DA_QK_WIDTH = 512
DA_WIDTH = 512
ROT_DIM = 16
ROPE_THETA = 500000.0
N_BRANCH = 3
BRANCH_WIDTH = 512
XA_HEADS = 4
XA_HEAD_DIM = 128
XA_WIDTH = 512
MLP_HIDDEN = 4096

LANES = 128
SUBLANES = 8
VMEM_LIMIT_BYTES = 58 * 1024 * 1024

N_PROJ_A = SSM_WIDTH + 3 * SB_WIDTH + 2 * DA_QK_WIDTH + DA_WIDTH
N_QKV = N_PROJ_A - SSM_WIDTH
SBQ_BLK, SBK_BLK, SBV_BLK = 0, 4, 8
DAQ_BLK, DAK_BLK, DAV_BLK = 12, 16, 20

SSM_WIN = SSM_WIDTH // LANES
SSM_WIN_STATE = (SSM_GROUPS // SSM_WIN) * SSM_STATE

LOG2_E = 1.4426950408889634
SB_LOG_CUTOFF = -150.0
MASK_VALUE = -1e30


def _rms(x, gain):
    ms = jnp.mean(x * x, axis=-1, keepdims=True)
    return x * lax.rsqrt(ms + RMS_EPS) * gain


def _dot(a, b):
    return jnp.dot(a, b, preferred_element_type=F32)


def _dot_nt(a, b):
    return lax.dot_general(a, b, (((1,), (1,)), ((), ())), preferred_element_type=F32)


def _params(*sem):
    return pltpu.CompilerParams(dimension_semantics=sem, vmem_limit_bytes=VMEM_LIMIT_BYTES)


def _const_spec(shape):
    nd = len(shape)
    return pl.BlockSpec(shape, lambda *_: (0,) * nd, pipeline_mode=pl.Buffered(1))


def _memkv_kernel(mem_ref, g_ref, w_ref, o_ref):
    h = _rms(mem_ref[0], g_ref[0]).astype(BF16)
    o_ref[0, 0] = _dot(h, w_ref[0]).astype(BF16)


def _memkv(mem, norm_mem, w_xkv_bf):
    depth = w_xkv_bf.shape[0]
    bsz = mem.shape[0]
    return pl.pallas_call(
        _memkv_kernel,
        grid=(depth, bsz),
        in_specs=[
            pl.BlockSpec((1, MEM_LEN, D_MODEL), lambda l, b: (b, 0, 0)),
            pl.BlockSpec((1, 1, D_MODEL), lambda l, b: (l, 0, 0)),
            pl.BlockSpec((1, D_MODEL, 2 * XA_WIDTH), lambda l, b: (l, 0, 0)),
        ],
        out_specs=pl.BlockSpec((1, 1, MEM_LEN, 2 * XA_WIDTH), lambda l, b: (l, b, 0, 0)),
        out_shape=jax.ShapeDtypeStruct((depth, bsz, MEM_LEN, 2 * XA_WIDTH), BF16),
        compiler_params=_params("arbitrary", "arbitrary"),
        name="memkv",
    )(mem, norm_mem.reshape(depth, 1, D_MODEL), w_xkv_bf)


def _inproj_ssm_kernel(x_ref, g_ref, wa_ref, cos_ref, sina_ref, sinb_ref, bw_ref, lr_ref, li_ref,
                       cw_ref, d_ref, wglu_ref, qkv_ref, yssm_ref, u_tb, bux, st, y_tb, *, ts, bsz):
    @pl.when(pl.program_id(0) == 0)
    def _():
        st[...] = jnp.zeros_like(st)

    m = ts * bsz
    h = _rms(x_ref[...].reshape(m, D_MODEL), g_ref[...]).astype(BF16)

    def proj(c0, width=SSM_WIDTH):
        return _dot(h, wa_ref[:, c0:c0 + width])

    def put(dst, val):
        qkv_ref[:, :, dst:dst + val.shape[1]] = val.astype(BF16).reshape(bsz, ts, val.shape[1])

    u = proj(0)
    for b in range(bsz):
        for w in range(SSM_WIN):
            u_tb[w, pl.ds(b, ts, stride=bsz), :] = u[ts * b:ts * (b + 1), LANES * w:LANES * (w + 1)]
    o = SSM_WIDTH
    put(0, proj(o) * (SB_HEAD_DIM ** -0.5 * LOG2_E))
    put(SB_WIDTH, proj(o + SB_WIDTH))
    put(2 * SB_WIDTH, proj(o + 2 * SB_WIDTH))
    cosf = cos_ref[...].reshape(m, LANES)
    sina = sina_ref[...].reshape(m, LANES)
    sinb = sinb_ref[...].reshape(m, LANES)
    src = o + 3 * SB_WIDTH
    dst = 3 * SB_WIDTH
    for scale in (DA_QK_DIM ** -0.5 * LOG2_E, 1.0):
        t4 = proj(src)
        for hh in range(DA_HEADS):
            t = t4[:, LANES * hh:LANES * (hh + 1)]
            r = (t * cosf + pltpu.roll(t, ROT_DIM // 2, 1) * sina
                 + pltpu.roll(t, LANES - ROT_DIM // 2, 1) * sinb)
            put(dst + LANES * hh, r * scale)
        src += DA_QK_WIDTH
        dst += DA_QK_WIDTH
    put(dst, proj(src))

    wst = 2 * SSM_WIN_STATE
    for w in range(SSM_WIN):
        bux[:, wst * w:wst * (w + 1)] = _dot(u_tb[w].astype(BF16), bw_ref[w])

    nhalf = 2 * SSM_WIN
    hs = SSM_WIN_STATE

    def body(t, carry):
        r = pl.multiple_of(t * bsz, bsz)
        new = []
        for w in range(SSM_WIN):
            xr, xi = carry[2 * w], carry[2 * w + 1]
            lr = lr_ref[:, hs * w:hs * (w + 1)]
            li = li_ref[:, hs * w:hs * (w + 1)]
            cr = slice(hs * 2 * w, hs * (2 * w + 1))
            ci = slice(hs * (2 * w + 1), hs * (2 * w + 2))
            nxr = lr * xr - li * xi + bux[pl.ds(r, bsz), cr]
            nxi = lr * xi + li * xr + bux[pl.ds(r, bsz), ci]
            bux[pl.ds(r, bsz), cr] = nxr
            bux[pl.ds(r, bsz), ci] = nxi
            new += [nxr, nxi]
        return tuple(new)

    fin = lax.fori_loop(0, ts, body, tuple(st[:, hs * k:hs * (k + 1)] for k in range(nhalf)), unroll=2)
    for k in range(nhalf):
        st[:, hs * k:hs * (k + 1)] = fin[k]

    ys = []
    for w in range(SSM_WIN):
        xw = bux[:, wst * w:wst * (w + 1)].astype(BF16)
        yw = _dot(xw, cw_ref[w])
        yw = yw + d_ref[:, LANES * w:LANES * (w + 1)] * u_tb[w]
        ys.append(jax.nn.gelu(yw).astype(BF16))
    glu = _dot(jnp.concatenate(ys, axis=1), wglu_ref[...])
    for w in range(SSM_WIN):
        sl = slice(LANES * w, LANES * (w + 1))
        y_tb[w] = glu[:, sl] * jax.nn.sigmoid(glu[:, SSM_WIDTH + LANES * w:SSM_WIDTH + LANES * (w + 1)])
    for b in range(bsz):
        for w in range(SSM_WIN):
            yssm_ref[b, :, LANES * w:LANES * (w + 1)] = y_tb[w, pl.ds(b, ts, stride=bsz), :].astype(BF16)


def _inproj_ssm(x, gain, wa, rope, ssm, d_skip, w_glu, *, ts):
    bsz, seq, _ = x.shape
    assert bsz == SUBLANES and seq % ts == 0
    cosf, sina, sinb = rope
    bw, lr, li, cw = ssm
    m = ts * bsz
    tile = lambda width: pl.BlockSpec((bsz, ts, width), lambda i: (0, i, 0))
    return pl.pallas_call(
        functools.partial(_inproj_ssm_kernel, ts=ts, bsz=bsz),
        grid=(seq // ts,),
        in_specs=[
            tile(D_MODEL),
            _const_spec((1, D_MODEL)),
            _const_spec((D_MODEL, N_PROJ_A)),
            tile(LANES), tile(LANES), tile(LANES),
            _const_spec(bw.shape), _const_spec(lr.shape), _const_spec(li.shape), _const_spec(cw.shape),
            _const_spec((1, SSM_WIDTH)),
            _const_spec((SSM_WIDTH, 2 * SSM_WIDTH)),
        ],
        out_specs=[tile(N_QKV), tile(SSM_WIDTH)],
        out_shape=[jax.ShapeDtypeStruct((bsz, seq, N_QKV), BF16),
                   jax.ShapeDtypeStruct((bsz, seq, SSM_WIDTH), BF16)],
        scratch_shapes=[
            pltpu.VMEM((SSM_WIN, m, LANES), F32),
            pltpu.VMEM((m, 2 * SSM_WIN * SSM_WIN_STATE), F32),
            pltpu.VMEM((bsz, 2 * SSM_WIN * SSM_WIN_STATE), F32),
            pltpu.VMEM((SSM_WIN, m, LANES), F32),
        ],
        compiler_params=_params("arbitrary"),
        name="inproj_ssm",
    )(x, gain.reshape(1, D_MODEL), wa, cosf, sina, sinb, bw, lr, li, cw,
      d_skip.reshape(1, SSM_WIDTH), w_glu)


def _sb_kernel(q_ref, k_ref, v_ref, o_ref, vt, *, tq):
    i = pl.program_id(2)
    tk = tq

    @pl.when(i == 0)
    def _():
        for c in range(vt.shape[0]):
            vt[c] = v_ref[0, tk * c:tk * (c + 1), :].astype(F32).T.astype(BF16)

    q2 = q_ref[0]
    lane = lax.broadcasted_iota(jnp.int32, (tq, LANES), 1)
    zero = jnp.zeros_like(q2)
    qs = (jnp.where(lane < SB_HEAD_DIM, q2, zero), jnp.where(lane >= SB_HEAD_DIM, q2, zero))
    key = lax.broadcasted_iota(jnp.int32, (tk, tq), 0)
    qry = lax.broadcasted_iota(jnp.int32, (tk, tq), 1)
    strict = key < qry
    tri = (lax.broadcasted_iota(jnp.int32, (tk, tk), 1)
           > lax.broadcasted_iota(jnp.int32, (tk, tk), 0)).astype(BF16)

    def blocks(js, masks, runs):
        kbs = [k_ref[0, pl.ds(pl.multiple_of(j * tk, tk), tk), :] for j in js]
        z = [[_dot_nt(kb, qs[a]) for a in range(2)] for kb in kbs]
        l1m, hi, lo = [], [], []
        for n, masked in enumerate(masks):
            row = []
            for a in range(2):
                t = -(jnp.maximum(z[n][a], 0.0) + jnp.log2(1.0 + jnp.exp2(-jnp.abs(z[n][a]))))
                row.append(jnp.where(strict, t, 0.0) if masked else t)
            l1m.append(row)
            hi.append([t.astype(BF16) for t in row])
            lo.append([(t - h.astype(F32)).astype(BF16) for t, h in zip(row, hi[n])])
        cum = [[_dot(tri, hi[n][a]) + _dot(tri, lo[n][a]) for a in range(2)] for n in range(len(js))]
        w = []
        runs = list(runs)
        for n, masked in enumerate(masks):
            row = []
            for a in range(2):
                t = jnp.exp2(z[n][a] + l1m[n][a] + (cum[n][a] + runs[a]))
                row.append((jnp.where(strict, t, 0.0) if masked else t).astype(BF16))
                runs[a] = runs[a] + jnp.sum(l1m[n][a], axis=0, keepdims=True)
            w.append(row)
        pvs = [[_dot(vt[j], w[n][a]) for a in range(2)] for n, j in enumerate(js)]
        return pvs, runs

    zrun = jnp.zeros((1, tq), F32)
    ((acc0, acc1), (p0, p1)), (run0, run1) = blocks((i, jnp.maximum(i - 1, 0)), (True, False), (zrun, zrun))
    live = i > 0
    acc0 = acc0 + jnp.where(live, p0, 0.0)
    acc1 = acc1 + jnp.where(live, p1, 0.0)

    def cond(c):
        j, r0, r1, _, _ = c
        return jnp.logical_and(j >= 0, jnp.max(jnp.maximum(r0, r1)) > SB_LOG_CUTOFF)

    def body(c):
        j, r0, r1, a0, a1 = c
        ((p0, p1),), (r0, r1) = blocks((j,), (False,), (r0, r1))
        return j - 1, r0, r1, a0 + p0, a1 + p1

    _, _, _, acc0, acc1 = lax.while_loop(cond, body, (i - 2, run0, run1, acc0, acc1))
    dim = lax.broadcasted_iota(jnp.int32, (LANES, tq), 0)
    o_ref[0] = jnp.where(dim < SB_HEAD_DIM, acc0, acc1).T.astype(BF16)


def _sb_attention(qkv, *, tq):
    bsz, seq, _ = qkv.shape
    npair = SB_WIDTH // LANES
    return pl.pallas_call(
        functools.partial(_sb_kernel, tq=tq),
        grid=(bsz, npair, seq // tq),
        in_specs=[
            pl.BlockSpec((1, tq, LANES), lambda b, p, i: (b, i, SBQ_BLK + p)),
            pl.BlockSpec((1, seq, LANES), lambda b, p, i: (b, 0, SBK_BLK + p)),
            pl.BlockSpec((1, seq, LANES), lambda b, p, i: (b, 0, SBV_BLK + p)),
        ],
        out_specs=pl.BlockSpec((1, tq, LANES), lambda b, p, i: (b, i, p)),
        out_shape=jax.ShapeDtypeStruct((bsz, seq, SB_WIDTH), BF16),
        scratch_shapes=[pltpu.VMEM((seq // tq, LANES, tq), BF16)],
        compiler_params=_params("arbitrary", "arbitrary", "arbitrary"),
        name="sb_attn",
    )(qkv, qkv, qkv)


def _da_kernel(q_ref, k_ref, v_ref, lam_ref, g_ref, o_ref, vt, sa, sb, acc, *, tq, th, lam_init):
    i = pl.program_id(2)

    @pl.when(i == 0)
    def _():
        for c in range(vt.shape[0]):
            vt[c] = v_ref[0, th * c:th * (c + 1), :].astype(F32).T.astype(BF16)

    q = q_ref[0]
    lane = lax.broadcasted_iota(jnp.int32, (tq, LANES), 1)
    zero = jnp.zeros_like(q)
    qs = (jnp.where(lane < DA_QK_DIM, q, zero), jnp.where(lane >= DA_QK_DIM, q, zero))

    def scores_to(buf, blk):
        ks = pl.multiple_of(blk * th, th)
        kb = k_ref[0, pl.ds(ks, th), :]
        for c in range(2):
            buf[c] = _dot_nt(kb, qs[c])

    def update(buf, blk, stats, masked):
        vtb = vt[blk]
        if masked:
            key = lax.broadcasted_iota(jnp.int32, (th, tq), 0) + blk * th
            qry = lax.broadcasted_iota(jnp.int32, (th, tq), 1) + i * tq
            causal = key <= qry
        out = []
        for c in range(2):
            m, l = stats[c]
            s = buf[c]
            if masked:
                s = jnp.where(causal, s, MASK_VALUE)
            mn = jnp.maximum(m, jnp.max(s, axis=0, keepdims=True))
            alpha = jnp.exp2(m - mn)
            p = jnp.exp2(s - mn)
            l = alpha * l + jnp.sum(p, axis=0, keepdims=True)
            acc[c] = alpha * acc[c] + _dot(vtb, p.astype(BF16))
            out.append((mn, l))
        return tuple(out)

    def pair(j, stats, masked):
        scores_to(sb, 2 * j + 1)
        stats = update(sa, 2 * j, stats, masked)
        if not masked:
            scores_to(sa, 2 * j + 2)
        return update(sb, 2 * j + 1, stats, masked)

    acc[...] = jnp.zeros_like(acc)
    scores_to(sa, 0)
    init = tuple((jnp.full((1, tq), MASK_VALUE, F32), jnp.zeros((1, tq), F32)) for _ in range(2))
    stats = lax.fori_loop(0, i, lambda j, st: pair(j, st, False), init)
    (_, l1), (_, l2) = pair(i, stats, True)

    lv = lam_ref[...]
    lam = (jnp.exp(jnp.sum(lv[0:1] * lv[1:2], axis=1, keepdims=True))
           - jnp.exp(jnp.sum(lv[2:3] * lv[3:4], axis=1, keepdims=True)) + lam_init)
    ot = acc[0] * (1.0 / l1) - acc[1] * (lam / l2)
    ms = jnp.mean(ot * ot, axis=0, keepdims=True)
    ot = ot * (lax.rsqrt(ms + RMS_EPS) * (1.0 - lam_init)) * g_ref[...]
    o_ref[0] = ot.T.astype(BF16)


def _da_attention(qkv, diff_lambda, diff_subln, lam_init, *, tq):
    bsz, seq, _ = qkv.shape
    th = tq // 2
    assert seq % tq == 0
    return pl.pallas_call(
        functools.partial(_da_kernel, tq=tq, th=th, lam_init=lam_init),
        grid=(bsz, DA_HEADS, seq // tq),
        in_specs=[
            pl.BlockSpec((1, tq, LANES), lambda b, h, i: (b, i, DAQ_BLK + h)),
            pl.BlockSpec((1, seq, LANES), lambda b, h, i: (b, 0, DAK_BLK + h)),
            pl.BlockSpec((1, seq, LANES), lambda b, h, i: (b, 0, DAV_BLK + h)),
            pl.BlockSpec((4, DA_QK_DIM), lambda b, h, i: (0, 0)),
            pl.BlockSpec((DA_V_DIM, 1), lambda b, h, i: (0, 0)),
        ],
        out_specs=pl.BlockSpec((1, tq, LANES), lambda b, h, i: (b, i, h)),
        out_shape=jax.ShapeDtypeStruct((bsz, seq, DA_WIDTH), BF16),
        scratch_shapes=[pltpu.VMEM((seq // th, DA_V_DIM, th), BF16),
                        pltpu.VMEM((2, th, tq), F32), pltpu.VMEM((2, th, tq), F32),
                        pltpu.VMEM((2, DA_V_DIM, tq), F32)],
        compiler_params=_params("arbitrary", "arbitrary", "arbitrary"),
        name="da_attn",
    )(qkv, qkv, qkv, diff_lambda, diff_subln.reshape(DA_V_DIM, 1))


def _merge_cross_kernel(x_ref, yssm_ref, ysb_ref, yda_ref, gmix_ref, wg_ref, wbr_ref, wout_ref,
                        gcross_ref, wxq_ref, kv_ref, wxo_ref, o_ref):
    x = x_ref[0]
    h = _rms(x, gmix_ref[...]).astype(BF16)
    merged = None
    for n, y_ref in enumerate((yssm_ref, ysb_ref, yda_ref)):
        gate = jax.nn.sigmoid(_dot(h, wg_ref[:, D_MODEL * n:D_MODEL * (n + 1)]))
        term = gate * _dot(y_ref[0], wbr_ref[n])
        merged = term if merged is None else merged + term
    x1 = x + _dot(merged.astype(BF16), wout_ref[...])

    hx = _rms(x1, gcross_ref[...]).astype(BF16)
    xq = _dot(hx, wxq_ref[...]).astype(BF16)
    kv = kv_ref[0]
    heads = []
    for hh in range(XA_HEADS):
        sl = slice(XA_HEAD_DIM * hh, XA_HEAD_DIM * (hh + 1))
        s = _dot_nt(xq[:, sl], kv[:, sl]) * XA_HEAD_DIM ** -0.5
        p = jnp.exp(s - jnp.max(s, axis=1, keepdims=True))
        vh = kv[:, XA_WIDTH + XA_HEAD_DIM * hh:XA_WIDTH + XA_HEAD_DIM * (hh + 1)]
        heads.append((_dot(p.astype(BF16), vh) / jnp.sum(p, axis=1, keepdims=True)).astype(BF16))
    o_ref[0] = x1 + _dot(jnp.concatenate(heads, axis=1), wxo_ref[...])


def _merge_cross(x, yssm, ysb, yda, gmix, wg, wbr, wout, gcross, wxq, kv, wxo, *, tm):
    bsz, seq, _ = x.shape
    tile = lambda width: pl.BlockSpec((1, tm, width), lambda b, i: (b, i, 0))
    return pl.pallas_call(
        _merge_cross_kernel,
        grid=(bsz, seq // tm),
        in_specs=[
            tile(D_MODEL), tile(BRANCH_WIDTH), tile(BRANCH_WIDTH), tile(BRANCH_WIDTH),
            _const_spec((1, D_MODEL)),
            _const_spec((D_MODEL, N_BRANCH * D_MODEL)),
            _const_spec((N_BRANCH, BRANCH_WIDTH, D_MODEL)),
            _const_spec((D_MODEL, D_MODEL)),
            _const_spec((1, D_MODEL)),
            _const_spec((D_MODEL, XA_WIDTH)),
            pl.BlockSpec((1, MEM_LEN, 2 * XA_WIDTH), lambda b, i: (b, 0, 0)),
            _const_spec((XA_WIDTH, D_MODEL)),
        ],
        out_specs=tile(D_MODEL),
        out_shape=jax.ShapeDtypeStruct((bsz, seq, D_MODEL), F32),
        compiler_params=_params("parallel", "arbitrary"),
        name="merge_cross",
    )(x, yssm, ysb, yda, gmix.reshape(1, D_MODEL), wg, wbr, wout, gcross.reshape(1, D_MODEL), wxq, kv, wxo)


def _mlp_kernel(x_ref, g_ref, wup_ref, wdown_ref, gfin_ref, o_ref, *, final, chunk):
    x = x_ref[...]
    h = _rms(x, g_ref[...]).astype(BF16)
    acc = x
    for c in range(MLP_HIDDEN // chunk):
        up = _dot(h, wup_ref[:, chunk * c:chunk * (c + 1)])
        act = jnp.square(jnp.maximum(up, 0.0)).astype(BF16)
        acc = acc + _dot(act, wdown_ref[chunk * c:chunk * (c + 1), :])
    if final:
        acc = _rms(acc, gfin_ref[...])
    o_ref[...] = acc


def _mlp(x2d, gain, wup, wdown, gfin, *, final, tm):
    n = x2d.shape[0]
    tile = pl.BlockSpec((tm, D_MODEL), lambda i: (i, 0))
    return pl.pallas_call(
        functools.partial(_mlp_kernel, final=final, chunk=1024),
        grid=(n // tm,),
        in_specs=[
            tile,
            _const_spec((1, D_MODEL)),
            _const_spec((D_MODEL, MLP_HIDDEN)),
            _const_spec((MLP_HIDDEN, D_MODEL)),
            _const_spec((1, D_MODEL)),
        ],
        out_specs=tile,
        out_shape=jax.ShapeDtypeStruct((n, D_MODEL), F32),
        compiler_params=_params("parallel"),
        name="mlp",
    )(x2d, gain.reshape(1, D_MODEL), wup, wdown, gfin.reshape(1, D_MODEL))


def _ssm_tables(lam_re, lam_im, log_dt, b_re, b_im, c_re, c_im, bsz):
    dt = jnp.exp(log_dt)[:, None]
    mag = jnp.exp(lam_re * dt)
    lbr = mag * jnp.cos(lam_im * dt)
    lbi = mag * jnp.sin(lam_im * dt)
    den = lam_re * lam_re + lam_im * lam_im
    fr = ((lbr - 1.0) * lam_re + lbi * lam_im) / den
    fi = (lbi * lam_re - (lbr - 1.0) * lam_im) / den
    bbr = fr[..., None] * b_re - fi[..., None] * b_im
    bbi = fr[..., None] * b_im + fi[..., None] * b_re
    gl = SSM_GROUPS // SSM_WIN
    eye = jnp.eye(gl, dtype=F32)
    shp = (SSM_WIN, gl, SSM_STATE, SSM_GROUP)
    to_b = lambda t: jnp.einsum("wgpc,gh->wgchp", t.reshape(shp), eye)
    bw = jnp.stack([to_b(bbr), to_b(bbi)], axis=3).reshape(SSM_WIN, LANES, 2 * SSM_WIN_STATE)
    shc = (SSM_WIN, gl, SSM_GROUP, SSM_STATE)
    to_c = lambda t: jnp.einsum("wgcp,gh->wgphc", t.reshape(shc), eye)
    cw = jnp.stack([to_c(c_re), -to_c(c_im)], axis=1).reshape(SSM_WIN, 2 * SSM_WIN_STATE, LANES)
    lr = jnp.broadcast_to(lbr.reshape(1, -1), (bsz, SSM_GROUPS * SSM_STATE))
    li = jnp.broadcast_to(lbi.reshape(1, -1), (bsz, SSM_GROUPS * SSM_STATE))
    return bw.astype(BF16), lr, li, cw.astype(BF16)


def _rope_tables(positions):
    half = ROT_DIM // 2
    inv_freq = ROPE_THETA ** (-jnp.arange(0, ROT_DIM, 2, dtype=F32) / ROT_DIM)
    ang = positions.astype(F32)[..., None] * inv_freq
    cos, sin = jnp.cos(ang), jnp.sin(ang)
    lead = ang.shape[:-1]
    ones = jnp.ones(lead + (DA_QK_DIM - ROT_DIM,), F32)
    zeros = jnp.zeros(lead + (DA_QK_DIM - ROT_DIM,), F32)
    zh = jnp.zeros(lead + (half,), F32)
    rep = lambda t: jnp.concatenate([t, t], axis=-1)
    cosf = rep(jnp.concatenate([cos, cos, ones], axis=-1))
    sina = rep(jnp.concatenate([zh, sin, zeros], axis=-1))
    sinb = rep(jnp.concatenate([-sin, zh, zeros], axis=-1))
    return cosf, sina, sinb


def kernel(x, mem, positions, norm_mix, w_in, ssm_lam_re, ssm_lam_im, ssm_log_dt, ssm_b_re, ssm_b_im, ssm_c_re, ssm_c_im, ssm_d, ssm_w_glu, diff_lambda, diff_subln, w_branch, w_out, norm_cross, norm_mem, w_xq, w_xkv, w_xo, norm_mlp, w_up, w_down, norm_final):
    bsz, seq, _ = x.shape
    depth = w_in.shape[0]
    ts = min(64, seq)
    sb_tq = min(256, seq)
    da_tq = min(512, seq)
    tm = min(512, seq)

    rope = _rope_tables(positions)
    kv_all = _memkv(mem, norm_mem, w_xkv.astype(BF16))

    for l in range(depth):
        wa = w_in[l, :, :N_PROJ_A].astype(BF16)
        wg = w_in[l, :, N_PROJ_A:].astype(BF16)
        ssm = _ssm_tables(ssm_lam_re[l], ssm_lam_im[l], ssm_log_dt[l], ssm_b_re[l], ssm_b_im[l],
                          ssm_c_re[l], ssm_c_im[l], bsz)
        qkv, yssm = _inproj_ssm(x, norm_mix[l], wa, rope, ssm, ssm_d[l], ssm_w_glu[l].astype(BF16), ts=ts)
        ysb = _sb_attention(qkv, tq=sb_tq)
        lam_init = 0.8 - 0.6 * math.exp(-0.3 * l)
        yda = _da_attention(qkv, diff_lambda[l], diff_subln[l], lam_init, tq=da_tq)
        x = _merge_cross(x, yssm, ysb, yda, norm_mix[l], wg, w_branch[l].astype(BF16), w_out[l].astype(BF16),
                         norm_cross[l], w_xq[l].astype(BF16), kv_all[l], w_xo[l].astype(BF16), tm=tm)
        x = _mlp(x.reshape(bsz * seq, D_MODEL), norm_mlp[l], w_up[l].astype(BF16), w_down[l].astype(BF16),
                 norm_final, final=(l == depth - 1), tm=tm).reshape(bsz, seq, D_MODEL)
    return x
```

```python
import functools
import math

import jax
import jax.numpy as jnp
from jax import lax
from jax.experimental import pallas as pl
from jax.experimental.pallas import tpu as pltpu

F32 = jnp.float32
BF16 = jnp.bfloat16

D_MODEL = 1024
MEM_LEN = 256
RMS_EPS = 1e-6
SSM_WIDTH = 512
SSM_GROUP = 16
SSM_GROUPS = 32
SSM_STATE = 64
SB_HEADS = 8
SB_HEAD_DIM = 64
SB_WIDTH = 512
DA_HEADS = 4
DA_QK_DIM = 64
DA_V_DIM = 128
DA_QK_WIDTH = 512
DA_WIDTH = 512
ROT_DIM = 16
ROPE_THETA = 500000.0
N_BRANCH = 3
BRANCH_WIDTH = 512
XA_HEADS = 4
XA_HEAD_DIM = 128
XA_WIDTH = 512
MLP_HIDDEN = 4096

LANES = 128
SUBLANES = 8
VMEM_LIMIT_BYTES = 58 * 1024 * 1024

N_PROJ_A = SSM_WIDTH + 3 * SB_WIDTH + 2 * DA_QK_WIDTH + DA_WIDTH
N_QKV = N_PROJ_A - SSM_WIDTH
SBQ_BLK, SBK_BLK, SBV_BLK = 0, 4, 8
DAQ_BLK, DAK_BLK, DAV_BLK = 12, 16, 20

SSM_WIN = SSM_WIDTH // LANES
SSM_WIN_STATE = (SSM_GROUPS // SSM_WIN) * SSM_STATE

LOG2_E = 1.4426950408889634
SB_LOG_CUTOFF = -150.0
MASK_VALUE = -1e30
DA_ONES_ROWS = 16


def _rms(x, gain):
    ms = jnp.mean(x * x, axis=-1, keepdims=True)
    return x * lax.rsqrt(ms + RMS_EPS) * gain


def _dot(a, b):
    return jnp.dot(a, b, preferred_element_type=F32)


def _dot_nt(a, b):
    return lax.dot_general(a, b, (((1,), (1,)), ((), ())), preferred_element_type=F32)


def _params(*sem):
    return pltpu.CompilerParams(dimension_semantics=sem, vmem_limit_bytes=VMEM_LIMIT_BYTES)


def _const_spec(shape):
    nd = len(shape)
    return pl.BlockSpec(shape, lambda *_: (0,) * nd, pipeline_mode=pl.Buffered(1))


def _layer_spec(shape, layer):
    nd = len(shape)
    return pl.BlockSpec((None,) + tuple(shape), lambda *_: (layer,) + (0,) * nd, pipeline_mode=pl.Buffered(1))


def _memkv_kernel(mem_ref, g_ref, w_ref, o_ref):
    h = _rms(mem_ref[0], g_ref[0]).astype(BF16)
    o_ref[0, 0] = _dot(h, w_ref[0]).astype(BF16)


def _memkv(mem, norm_mem, w_xkv_bf):
    depth = w_xkv_bf.shape[0]
    bsz = mem.shape[0]
    return pl.pallas_call(
        _memkv_kernel,
        grid=(depth, bsz),
        in_specs=[
            pl.BlockSpec((1, MEM_LEN, D_MODEL), lambda l, b: (b, 0, 0)),
            pl.BlockSpec((1, 1, D_MODEL), lambda l, b: (l, 0, 0)),
            pl.BlockSpec((1, D_MODEL, 2 * XA_WIDTH), lambda l, b: (l, 0, 0)),
        ],
        out_specs=pl.BlockSpec((1, 1, MEM_LEN, 2 * XA_WIDTH), lambda l, b: (l, b, 0, 0)),
        out_shape=jax.ShapeDtypeStruct((depth, bsz, MEM_LEN, 2 * XA_WIDTH), BF16),
        compiler_params=_params("arbitrary", "arbitrary"),
        name="memkv",
    )(mem, norm_mem.reshape(depth, 1, D_MODEL), w_xkv_bf)


def _inproj_ssm_kernel(x_ref, g_ref, wa_ref, cos_ref, sina_ref, sinb_ref, bw_ref, lr_ref, li_ref,
                       cw_ref, d_ref, wglu_ref, qkv_ref, yssm_ref, u_tb, bux, st, y_tb, *, ts, bsz):
    @pl.when(pl.program_id(0) == 0)
    def _():
        st[...] = jnp.zeros_like(st)

    m = ts * bsz
    h = _rms(x_ref[...].reshape(m, D_MODEL), g_ref[...]).astype(BF16)

    def proj(c0, width=SSM_WIDTH):
        return _dot(h, wa_ref[:, c0:c0 + width])

    def put(dst, val):
        qkv_ref[:, :, dst:dst + val.shape[1]] = val.astype(BF16).reshape(bsz, ts, val.shape[1])

    u = proj(0)
    for b in range(bsz):
        for w in range(SSM_WIN):
            u_tb[w, pl.ds(b, ts, stride=bsz), :] = u[ts * b:ts * (b + 1), LANES * w:LANES * (w + 1)]
    o = SSM_WIDTH
    put(0, proj(o) * (SB_HEAD_DIM ** -0.5 * LOG2_E))
    put(SB_WIDTH, proj(o + SB_WIDTH))
    put(2 * SB_WIDTH, proj(o + 2 * SB_WIDTH))
    cosf = cos_ref[...].reshape(m, LANES)
    sina = sina_ref[...].reshape(m, LANES)
    sinb = sinb_ref[...].reshape(m, LANES)
    src = o + 3 * SB_WIDTH
    dst = 3 * SB_WIDTH
    for scale in (DA_QK_DIM ** -0.5 * LOG2_E, 1.0):
        t4 = proj(src)
        for hh in range(DA_HEADS):
            t = t4[:, LANES * hh:LANES * (hh + 1)]
            r = (t * cosf + pltpu.roll(t, ROT_DIM // 2, 1) * sina
                 + pltpu.roll(t, LANES - ROT_DIM // 2, 1) * sinb)
            put(dst + LANES * hh, r * scale)
        src += DA_QK_WIDTH
        dst += DA_QK_WIDTH
    put(dst, proj(src))

    wst = 2 * SSM_WIN_STATE
    for w in range(SSM_WIN):
        bux[:, wst * w:wst * (w + 1)] = _dot(u_tb[w].astype(BF16), bw_ref[w])

    nhalf = 2 * SSM_WIN
    hs = SSM_WIN_STATE

    def body(t, carry):
        r = pl.multiple_of(t * bsz, bsz)
        new = []
        for w in range(SSM_WIN):
            xr, xi = carry[2 * w], carry[2 * w + 1]
            lr = lr_ref[:, hs * w:hs * (w + 1)]
            li = li_ref[:, hs * w:hs * (w + 1)]
            cr = slice(hs * 2 * w, hs * (2 * w + 1))
            ci = slice(hs * (2 * w + 1), hs * (2 * w + 2))
            nxr = lr * xr - li * xi + bux[pl.ds(r, bsz), cr]
            nxi = lr * xi + li * xr + bux[pl.ds(r, bsz), ci]
            bux[pl.ds(r, bsz), cr] = nxr
            bux[pl.ds(r, bsz), ci] = nxi
            new += [nxr, nxi]
        return tuple(new)

    fin = lax.fori_loop(0, ts, body, tuple(st[:, hs * k:hs * (k + 1)] for k in range(nhalf)), unroll=2)
    for k in range(nhalf):
        st[:, hs * k:hs * (k + 1)] = fin[k]

    ys = []
    for w in range(SSM_WIN):
        xw = bux[:, wst * w:wst * (w + 1)].astype(BF16)
        yw = _dot(xw, cw_ref[w])
        yw = yw + d_ref[:, LANES * w:LANES * (w + 1)] * u_tb[w]
        ys.append(jax.nn.gelu(yw).astype(BF16))
    glu = _dot(jnp.concatenate(ys, axis=1), wglu_ref[...])
    for w in range(SSM_WIN):
        sl = slice(LANES * w, LANES * (w + 1))
        y_tb[w] = glu[:, sl] * jax.nn.sigmoid(glu[:, SSM_WIDTH + LANES * w:SSM_WIDTH + LANES * (w + 1)])
    for b in range(bsz):
        for w in range(SSM_WIN):
            yssm_ref[b, :, LANES * w:LANES * (w + 1)] = y_tb[w, pl.ds(b, ts, stride=bsz), :].astype(BF16)


def _inproj_ssm(x, gain, wa, rope, ssm, d_skip, w_glu, layer, *, ts):
    bsz, seq, _ = x.shape
    assert bsz == SUBLANES and seq % ts == 0
    cosf, sina, sinb = rope
    bw, lr, li, cw = ssm
    m = ts * bsz
    tile = lambda width: pl.BlockSpec((bsz, ts, width), lambda i: (0, i, 0))
    return pl.pallas_call(
        functools.partial(_inproj_ssm_kernel, ts=ts, bsz=bsz),
        grid=(seq // ts,),
        in_specs=[
            tile(D_MODEL),
            _const_spec((1, D_MODEL)),
            _layer_spec((D_MODEL, N_PROJ_A), layer),
            tile(LANES), tile(LANES), tile(LANES),
            _const_spec(bw.shape), _const_spec(lr.shape), _const_spec(li.shape), _const_spec(cw.shape),
            _const_spec((1, SSM_WIDTH)),
            _layer_spec((SSM_WIDTH, 2 * SSM_WIDTH), layer),
        ],
        out_specs=[tile(N_QKV), tile(SSM_WIDTH)],
        out_shape=[jax.ShapeDtypeStruct((bsz, seq, N_QKV), BF16),
                   jax.ShapeDtypeStruct((bsz, seq, SSM_WIDTH), BF16)],
        scratch_shapes=[
            pltpu.VMEM((SSM_WIN, m, LANES), F32),
            pltpu.VMEM((m, 2 * SSM_WIN * SSM_WIN_STATE), F32),
            pltpu.VMEM((bsz, 2 * SSM_WIN * SSM_WIN_STATE), F32),
            pltpu.VMEM((SSM_WIN, m, LANES), F32),
        ],
        compiler_params=_params("arbitrary"),
        name="inproj_ssm",
    )(x, gain.reshape(1, D_MODEL), wa, cosf, sina, sinb, bw, lr, li, cw,
      d_skip.reshape(1, SSM_WIDTH), w_glu)


def _sb_kernel(q_ref, k_ref, v_ref, o_ref, vt, *, tq):
    i = pl.program_id(2)
    tk = tq

    @pl.when(i == 0)
    def _():
        for c in range(vt.shape[0]):
            vt[c] = v_ref[0, tk * c:tk * (c + 1), :].astype(F32).T.astype(BF16)

    q2 = q_ref[0]
    lane = lax.broadcasted_iota(jnp.int32, (tq, LANES), 1)
    zero = jnp.zeros_like(q2)
    qs = (jnp.where(lane < SB_HEAD_DIM, q2, zero), jnp.where(lane >= SB_HEAD_DIM, q2, zero))
    key = lax.broadcasted_iota(jnp.int32, (tk, tq), 0)
    qry = lax.broadcasted_iota(jnp.int32, (tk, tq), 1)
    strict = key < qry
    tri = (lax.broadcasted_iota(jnp.int32, (tk, tk), 1)
           > lax.broadcasted_iota(jnp.int32, (tk, tk), 0)).astype(BF16)

    def blocks(js, masks, runs):
        kbs = [k_ref[0, pl.ds(pl.multiple_of(j * tk, tk), tk), :] for j in js]
        z = [[_dot_nt(kb, qs[a]) for a in range(2)] for kb in kbs]
        l1m, hi = [], []
        for n, masked in enumerate(masks):
            row = []
            for a in range(2):
                t = -(jnp.maximum(z[n][a], 0.0) + jnp.log2(1.0 + jnp.exp2(-jnp.abs(z[n][a]))))
                row.append(jnp.where(strict, t, 0.0) if masked else t)
            l1m.append(row)
            hi.append([t.astype(BF16) for t in row])
        cum = [[_dot(tri, hi[n][a]) for a in range(2)] for n in range(len(js))]
        w = []
        runs = list(runs)
        for n, masked in enumerate(masks):
            row = []
            for a in range(2):
                t = jnp.exp2(z[n][a] + l1m[n][a] + (cum[n][a] + runs[a]))
                row.append((jnp.where(strict, t, 0.0) if masked else t).astype(BF16))
                runs[a] = runs[a] + jnp.sum(l1m[n][a], axis=0, keepdims=True)
            w.append(row)
        pvs = [[_dot(vt[j], w[n][a]) for a in range(2)] for n, j in enumerate(js)]
        return pvs, runs

    zrun = jnp.zeros((1, tq), F32)
    ((acc0, acc1), (p0, p1)), (run0, run1) = blocks((i, jnp.maximum(i - 1, 0)), (True, False), (zrun, zrun))
    live = i > 0
    acc0 = acc0 + jnp.where(live, p0, 0.0)
    acc1 = acc1 + jnp.where(live, p1, 0.0)

    def cond(c):
        j, r0, r1, _, _ = c
        return jnp.logical_and(j >= 0, jnp.max(jnp.maximum(r0, r1)) > SB_LOG_CUTOFF)

    def body(c):
        j, r0, r1, a0, a1 = c
        ((p0, p1),), (r0, r1) = blocks((j,), (False,), (r0, r1))
        return j - 1, r0, r1, a0 + p0, a1 + p1

    _, _, _, acc0, acc1 = lax.while_loop(cond, body, (i - 2, run0, run1, acc0, acc1))
    dim = lax.broadcasted_iota(jnp.int32, (LANES, tq), 0)
    o_ref[0] = jnp.where(dim < SB_HEAD_DIM, acc0, acc1).T.astype(BF16)


def _sb_attention(qkv, *, tq):
    bsz, seq, _ = qkv.shape
    npair = SB_WIDTH // LANES
    return pl.pallas_call(
        functools.partial(_sb_kernel, tq=tq),
        grid=(bsz, npair, seq // tq),
        in_specs=[
            pl.BlockSpec((1, tq, LANES), lambda b, p, i: (b, i, SBQ_BLK + p)),
            pl.BlockSpec((1, seq, LANES), lambda b, p, i: (b, 0, SBK_BLK + p)),
            pl.BlockSpec((1, seq, LANES), lambda b, p, i: (b, 0, SBV_BLK + p)),
        ],
        out_specs=pl.BlockSpec((1, tq, LANES), lambda b, p, i: (b, i, p)),
        out_shape=jax.ShapeDtypeStruct((bsz, seq, SB_WIDTH), BF16),
        scratch_shapes=[pltpu.VMEM((seq // tq, LANES, tq), BF16)],
        compiler_params=_params("arbitrary", "arbitrary", "arbitrary"),
        name="sb_attn",
    )(qkv, qkv, qkv)


def _da_kernel(q_ref, k_ref, v_ref, lam_ref, g_ref, o_ref, vt, sa, sb, acc, *, tq, th, lam_init):
    i = pl.program_id(2)

    @pl.when(i == 0)
    def _():
        for c in range(vt.shape[0]):
            vt[c, :DA_V_DIM, :] = v_ref[0, th * c:th * (c + 1), :].astype(F32).T.astype(BF16)
            vt[c, DA_V_DIM:, :] = jnp.ones((DA_ONES_ROWS, th), BF16)

    q = q_ref[0]
    lane = lax.broadcasted_iota(jnp.int32, (tq, LANES), 1)
    zero = jnp.zeros_like(q)
    qs = (jnp.where(lane < DA_QK_DIM, q, zero), jnp.where(lane >= DA_QK_DIM, q, zero))

    def scores_to(buf, blk):
        ks = pl.multiple_of(blk * th, th)
        kb = k_ref[0, pl.ds(ks, th), :]
        for c in range(2):
            buf[c] = _dot_nt(kb, qs[c])

    def update(buf, blk, stats, masked):
        vtb = vt[blk]
        if masked:
            key = lax.broadcasted_iota(jnp.int32, (th, tq), 0) + blk * th
            qry = lax.broadcasted_iota(jnp.int32, (th, tq), 1) + i * tq
            causal = key <= qry
        out = []
        for c in range(2):
            m = stats[c]
            s = buf[c]
            if masked:
                s = jnp.where(causal, s, MASK_VALUE)
            mn = jnp.maximum(m, jnp.max(s, axis=0, keepdims=True))
            alpha = jnp.exp2(m - mn)
            p = jnp.exp2(s - mn).astype(BF16)
            acc[c] = alpha * acc[c] + _dot(vtb, p)
            out.append(mn)
        return tuple(out)

    def pair(j, stats, masked):
        scores_to(sb, 2 * j + 1)
        stats = update(sa, 2 * j, stats, masked)
        if not masked:
            scores_to(sa, 2 * j + 2)
        return update(sb, 2 * j + 1, stats, masked)

    acc[...] = jnp.zeros_like(acc)
    scores_to(sa, 0)
    init = tuple(jnp.full((1, tq), MASK_VALUE, F32) for _ in range(2))
    stats = lax.fori_loop(0, i, lambda j, st: pair(j, st, False), init)
    pair(i, stats, True)
    l1 = acc[0, DA_V_DIM:DA_V_DIM + 1, :]
    l2 = acc[1, DA_V_DIM:DA_V_DIM + 1, :]

    lv = lam_ref[...]
    lam = (jnp.exp(jnp.sum(lv[0:1] * lv[1:2], axis=1, keepdims=True))
           - jnp.exp(jnp.sum(lv[2:3] * lv[3:4], axis=1, keepdims=True)) + lam_init)
    ot = acc[0, :DA_V_DIM, :] * (1.0 / l1) - acc[1, :DA_V_DIM, :] * (lam / l2)
    ms = jnp.mean(ot * ot, axis=0, keepdims=True)
    ot = ot * (lax.rsqrt(ms + RMS_EPS) * (1.0 - lam_init)) * g_ref[...]
    o_ref[0] = ot.T.astype(BF16)


def _da_attention(qkv, diff_lambda, diff_subln, lam_init, *, tq):
    bsz, seq, _ = qkv.shape
    th = tq // 2
    assert seq % tq == 0
    return pl.pallas_call(
        functools.partial(_da_kernel, tq=tq, th=th, lam_init=lam_init),
        grid=(bsz, DA_HEADS, seq // tq),
        in_specs=[
            pl.BlockSpec((1, tq, LANES), lambda b, h, i: (b, i, DAQ_BLK + h)),
            pl.BlockSpec((1, seq, LANES), lambda b, h, i: (b, 0, DAK_BLK + h)),
            pl.BlockSpec((1, seq, LANES), lambda b, h, i: (b, 0, DAV_BLK + h)),
            pl.BlockSpec((4, DA_QK_DIM), lambda b, h, i: (0, 0)),
            pl.BlockSpec((DA_V_DIM, 1), lambda b, h, i: (0, 0)),
        ],
        out_specs=pl.BlockSpec((1, tq, LANES), lambda b, h, i: (b, i, h)),
        out_shape=jax.ShapeDtypeStruct((bsz, seq, DA_WIDTH), BF16),
        scratch_shapes=[pltpu.VMEM((seq // th, DA_V_DIM + DA_ONES_ROWS, th), BF16),
                        pltpu.VMEM((2, th, tq), F32), pltpu.VMEM((2, th, tq), F32),
                        pltpu.VMEM((2, DA_V_DIM + DA_ONES_ROWS, tq), F32)],
        compiler_params=_params("arbitrary", "arbitrary", "arbitrary"),
        name="da_attn",
    )(qkv, qkv, qkv, diff_lambda, diff_subln.reshape(DA_V_DIM, 1))


def _merge_cross_kernel(x_ref, yssm_ref, ysb_ref, yda_ref, gmix_ref, wg_ref, wbr_ref, wout_ref,
                        gcross_ref, wxq_ref, kv_ref, wxo_ref, o_ref):
    x = x_ref[0]
    h = _rms(x, gmix_ref[...]).astype(BF16)
    merged = None
    for n, y_ref in enumerate((yssm_ref, ysb_ref, yda_ref)):
        gate = jax.nn.sigmoid(_dot(h, wg_ref[:, D_MODEL * n:D_MODEL * (n + 1)]))
        term = gate * _dot(y_ref[0], wbr_ref[n])
        merged = term if merged is None else merged + term
    x1 = x + _dot(merged.astype(BF16), wout_ref[...])

    hx = _rms(x1, gcross_ref[...]).astype(BF16)
    xq = _dot(hx, wxq_ref[...]).astype(BF16)
    kv = kv_ref[0]
    heads = []
    for hh in range(XA_HEADS):
        sl = slice(XA_HEAD_DIM * hh, XA_HEAD_DIM * (hh + 1))
        s = _dot_nt(xq[:, sl], kv[:, sl]) * XA_HEAD_DIM ** -0.5
        p = jnp.exp(s - jnp.max(s, axis=1, keepdims=True))
        vh = kv[:, XA_WIDTH + XA_HEAD_DIM * hh:XA_WIDTH + XA_HEAD_DIM * (hh + 1)]
        heads.append((_dot(p.astype(BF16), vh) / jnp.sum(p, axis=1, keepdims=True)).astype(BF16))
    o_ref[0] = x1 + _dot(jnp.concatenate(heads, axis=1), wxo_ref[...])


def _merge_cross(x, yssm, ysb, yda, gmix, wg, wbr, wout, gcross, wxq, kv, wxo, layer, *, tm):
    bsz, seq, _ = x.shape
    tile = lambda width: pl.BlockSpec((1, tm, width), lambda b, i: (b, i, 0))
    return pl.pallas_call(
        _merge_cross_kernel,
        grid=(bsz, seq // tm),
        in_specs=[
            tile(D_MODEL), tile(BRANCH_WIDTH), tile(BRANCH_WIDTH), tile(BRANCH_WIDTH),
            _const_spec((1, D_MODEL)),
            _layer_spec((D_MODEL, N_BRANCH * D_MODEL), layer),
            _layer_spec((N_BRANCH, BRANCH_WIDTH, D_MODEL), layer),
            _layer_spec((D_MODEL, D_MODEL), layer),
            _const_spec((1, D_MODEL)),
            _layer_spec((D_MODEL, XA_WIDTH), layer),
            pl.BlockSpec((None, 1, MEM_LEN, 2 * XA_WIDTH), lambda b, i: (layer, b, 0, 0)),
            _layer_spec((XA_WIDTH, D_MODEL), layer),
        ],
        out_specs=tile(D_MODEL),
        out_shape=jax.ShapeDtypeStruct((bsz, seq, D_MODEL), F32),
        compiler_params=_params("parallel", "arbitrary"),
        name="merge_cross",
    )(x, yssm, ysb, yda, gmix.reshape(1, D_MODEL), wg, wbr, wout, gcross.reshape(1, D_MODEL), wxq, kv, wxo)


def _mlp_kernel(x_ref, g_ref, wup_ref, wdown_ref, gfin_ref, o_ref, *, final, chunk):
    x = x_ref[...]
    h = _rms(x, g_ref[...]).astype(BF16)
    acc = x
    for c in range(MLP_HIDDEN // chunk):
        up = _dot(h, wup_ref[:, chunk * c:chunk * (c + 1)])
        act = jnp.square(jnp.maximum(up, 0.0)).astype(BF16)
        acc = acc + _dot(act, wdown_ref[chunk * c:chunk * (c + 1), :])
    if final:
        acc = _rms(acc, gfin_ref[...])
    o_ref[...] = acc


def _mlp(x2d, gain, wup, wdown, gfin, layer, *, final, tm):
    n = x2d.shape[0]
    tile = pl.BlockSpec((tm, D_MODEL), lambda i: (i, 0))
    return pl.pallas_call(
        functools.partial(_mlp_kernel, final=final, chunk=1024),
        grid=(n // tm,),
        in_specs=[
            tile,
            _const_spec((1, D_MODEL)),
            _layer_spec((D_MODEL, MLP_HIDDEN), layer),
            _layer_spec((MLP_HIDDEN, D_MODEL), layer),
            _const_spec((1, D_MODEL)),
        ],
        out_specs=tile,
        out_shape=jax.ShapeDtypeStruct((n, D_MODEL), F32),
        compiler_params=_params("parallel"),
        name="mlp",
    )(x2d, gain.reshape(1, D_MODEL), wup, wdown, gfin.reshape(1, D_MODEL))


def _ssm_tables(lam_re, lam_im, log_dt, b_re, b_im, c_re, c_im, bsz):
    dt = jnp.exp(log_dt)[:, None]
    mag = jnp.exp(lam_re * dt)
    lbr = mag * jnp.cos(lam_im * dt)
    lbi = mag * jnp.sin(lam_im * dt)
    den = lam_re * lam_re + lam_im * lam_im
    fr = ((lbr - 1.0) * lam_re + lbi * lam_im) / den
    fi = (lbi * lam_re - (lbr - 1.0) * lam_im) / den
    bbr = fr[..., None] * b_re - fi[..., None] * b_im
    bbi = fr[..., None] * b_im + fi[..., None] * b_re
    gl = SSM_GROUPS // SSM_WIN
    eye = jnp.eye(gl, dtype=F32)
    shp = (SSM_WIN, gl, SSM_STATE, SSM_GROUP)
    to_b = lambda t: jnp.einsum("wgpc,gh->wgchp", t.reshape(shp), eye)
    bw = jnp.stack([to_b(bbr), to_b(bbi)], axis=3).reshape(SSM_WIN, LANES, 2 * SSM_WIN_STATE)
    shc = (SSM_WIN, gl, SSM_GROUP, SSM_STATE)
    to_c = lambda t: jnp.einsum("wgcp,gh->wgphc", t.reshape(shc), eye)
    cw = jnp.stack([to_c(c_re), -to_c(c_im)], axis=1).reshape(SSM_WIN, 2 * SSM_WIN_STATE, LANES)
    lr = jnp.broadcast_to(lbr.reshape(1, -1), (bsz, SSM_GROUPS * SSM_STATE))
    li = jnp.broadcast_to(lbi.reshape(1, -1), (bsz, SSM_GROUPS * SSM_STATE))
    return bw.astype(BF16), lr, li, cw.astype(BF16)


def _rope_tables(positions):
    half = ROT_DIM // 2
    lane = jnp.arange(LANES, dtype=jnp.int32) % DA_QK_DIM
    inv_freq = ROPE_THETA ** (-(2 * (lane % half)).astype(F32) / ROT_DIM)
    pos = positions.astype(F32)[..., None]
    cosf = jnp.cos(pos * jnp.where(lane < ROT_DIM, inv_freq, 0.0))
    sina = jnp.sin(pos * jnp.where((lane >= half) & (lane < ROT_DIM), inv_freq, 0.0))
    sinb = -jnp.sin(pos * jnp.where(lane < half, inv_freq, 0.0))
    return cosf, sina, sinb


def kernel(x, mem, positions, norm_mix, w_in, ssm_lam_re, ssm_lam_im, ssm_log_dt, ssm_b_re, ssm_b_im, ssm_c_re, ssm_c_im, ssm_d, ssm_w_glu, diff_lambda, diff_subln, w_branch, w_out, norm_cross, norm_mem, w_xq, w_xkv, w_xo, norm_mlp, w_up, w_down, norm_final):
    bsz, seq, _ = x.shape
    depth = w_in.shape[0]
    ts = min(64, seq)
    sb_tq = min(256, seq)
    da_tq = min(512, seq)
    tm = min(512, seq)

    rope = _rope_tables(positions)
    kv_all = _memkv(mem, norm_mem, w_xkv.astype(BF16))
    wa = w_in[:, :, :N_PROJ_A].astype(BF16)
    wg = w_in[:, :, N_PROJ_A:].astype(BF16)
    wglu, wbr, wout = ssm_w_glu.astype(BF16), w_branch.astype(BF16), w_out.astype(BF16)
    wxq, wxo, wup, wdown = w_xq.astype(BF16), w_xo.astype(BF16), w_up.astype(BF16), w_down.astype(BF16)

    for l in range(depth):
        ssm = _ssm_tables(ssm_lam_re[l], ssm_lam_im[l], ssm_log_dt[l], ssm_b_re[l], ssm_b_im[l],
                          ssm_c_re[l], ssm_c_im[l], bsz)
        qkv, yssm = _inproj_ssm(x, norm_mix[l], wa, rope, ssm, ssm_d[l], wglu, l, ts=ts)
        ysb = _sb_attention(qkv, tq=sb_tq)
        lam_init = 0.8 - 0.6 * math.exp(-0.3 * l)
        yda = _da_attention(qkv, diff_lambda[l], diff_subln[l], lam_init, tq=da_tq)
        x = _merge_cross(x, yssm, ysb, yda, norm_mix[l], wg, wbr, wout, norm_cross[l], wxq, kv_all, wxo, l, tm=tm)
        x = _mlp(x.reshape(bsz * seq, D_MODEL), norm_mlp[l], wup, wdown, norm_final, l,
                 final=(l == depth - 1), tm=tm).reshape(bsz, seq, D_MODEL)
    return x
```

```python
import functools
import math

import jax
import jax.numpy as jnp
import numpy as np
from jax import lax
from jax.experimental import pallas as pl
from jax.experimental.pallas import tpu as pltpu

F32 = jnp.float32
BF16 = jnp.bfloat16

D_MODEL = 1024
MEM_LEN = 256
RMS_EPS = 1e-6
SSM_WIDTH = 512
SSM_GROUP = 16
SSM_GROUPS = 32
SSM_STATE = 64
SB_HEADS = 8
SB_HEAD_DIM = 64
SB_WIDTH = 512
DA_HEADS = 4
DA_QK_DIM = 64
DA_V_DIM = 128
DA_QK_WIDTH = 512
DA_WIDTH = 512
ROT_DIM = 16
ROPE_THETA = 500000.0
N_BRANCH = 3
BRANCH_WIDTH = 512
XA_HEADS = 4
XA_HEAD_DIM = 128
XA_WIDTH = 512
MLP_HIDDEN = 4096

LANES = 128
SUBLANES = 8
VMEM_LIMIT_BYTES = 58 * 1024 * 1024

N_PROJ_A = SSM_WIDTH + 3 * SB_WIDTH + 2 * DA_QK_WIDTH + DA_WIDTH
N_QKV = N_PROJ_A - SSM_WIDTH
SBQ_BLK, SBK_BLK, SBV_BLK = 0, 4, 8
DAQ_BLK, DAK_BLK, DAV_BLK = 12, 16, 20

SSM_WIN = SSM_WIDTH // LANES
SSM_WIN_STATE = (SSM_GROUPS // SSM_WIN) * SSM_STATE

LOG2_E = 1.4426950408889634
SB_LOG_CUTOFF = -150.0
MASK_VALUE = -1e30
DA_ONES_ROWS = 16


def _rms(x, gain):
    ms = jnp.mean(x * x, axis=-1, keepdims=True)
    return x * lax.rsqrt(ms + RMS_EPS) * gain


def _dot(a, b):
    return jnp.dot(a, b, preferred_element_type=F32)


def _dot_nt(a, b):
    return lax.dot_general(a, b, (((1,), (1,)), ((), ())), preferred_element_type=F32)


def _params(*sem):
    return pltpu.CompilerParams(dimension_semantics=sem, vmem_limit_bytes=VMEM_LIMIT_BYTES)


def _const_spec(shape):
    nd = len(shape)
    return pl.BlockSpec(shape, lambda *_: (0,) * nd, pipeline_mode=pl.Buffered(1))


def _layer_spec(shape, layer):
    nd = len(shape)
    return pl.BlockSpec((None,) + tuple(shape), lambda *_: (layer,) + (0,) * nd, pipeline_mode=pl.Buffered(1))


def _memkv_kernel(mem_ref, g_ref, w_ref, o_ref):
    h = _rms(mem_ref[0], g_ref[0]).astype(BF16)
    o_ref[0, 0] = _dot(h, w_ref[0]).astype(BF16)


def _memkv(mem, norm_mem, w_xkv_bf):
    depth = w_xkv_bf.shape[0]
    bsz = mem.shape[0]
    return pl.pallas_call(
        _memkv_kernel,
        grid=(depth, bsz),
        in_specs=[
            pl.BlockSpec((1, MEM_LEN, D_MODEL), lambda l, b: (b, 0, 0)),
            pl.BlockSpec((1, 1, D_MODEL), lambda l, b: (l, 0, 0)),
            pl.BlockSpec((1, D_MODEL, 2 * XA_WIDTH), lambda l, b: (l, 0, 0)),
        ],
        out_specs=pl.BlockSpec((1, 1, MEM_LEN, 2 * XA_WIDTH), lambda l, b: (l, b, 0, 0)),
        out_shape=jax.ShapeDtypeStruct((depth, bsz, MEM_LEN, 2 * XA_WIDTH), BF16),
        compiler_params=_params("arbitrary", "arbitrary"),
        name="memkv",
    )(mem, norm_mem.reshape(depth, 1, D_MODEL), w_xkv_bf)


def _inproj_ssm_kernel(x_ref, g_ref, wa_ref, cos_ref, sina_ref, sinb_ref, bw_ref, lr_ref, li_ref,
                       cw_ref, d_ref, wglu_ref, qkv_ref, yssm_ref, u_tb, bux, st, y_tb, *, ts, bsz):
    @pl.when(pl.program_id(0) == 0)
    def _():
        st[...] = jnp.zeros_like(st)

    m = ts * bsz
    h = _rms(x_ref[...].reshape(m, D_MODEL), g_ref[...]).astype(BF16)

    def proj(c0, width=SSM_WIDTH):
        return _dot(h, wa_ref[:, c0:c0 + width])

    def put(dst, val):
        qkv_ref[:, :, dst:dst + val.shape[1]] = val.astype(BF16).reshape(bsz, ts, val.shape[1])

    u = proj(0)
    for b in range(bsz):
        for w in range(SSM_WIN):
            u_tb[w, pl.ds(b, ts, stride=bsz), :] = u[ts * b:ts * (b + 1), LANES * w:LANES * (w + 1)]

    wst = 2 * SSM_WIN_STATE
    for w in range(SSM_WIN):
        bux[:, wst * w:wst * (w + 1)] = _dot(u_tb[w].astype(BF16), bw_ref[w])

    nhalf = 2 * SSM_WIN
    hs = SSM_WIN_STATE
    state = [st[:, hs * k:hs * (k + 1)] for k in range(nhalf)]
    for t in range(ts):
        rows = slice(bsz * t, bsz * (t + 1))
        for w in range(SSM_WIN):
            xr, xi = state[2 * w], state[2 * w + 1]
            lr = lr_ref[:, hs * w:hs * (w + 1)]
            li = li_ref[:, hs * w:hs * (w + 1)]
            cr = slice(hs * 2 * w, hs * (2 * w + 1))
            ci = slice(hs * (2 * w + 1), hs * (2 * w + 2))
            state[2 * w] = lr * xr - li * xi + bux[rows, cr]
            state[2 * w + 1] = lr * xi + li * xr + bux[rows, ci]
            bux[rows, cr] = state[2 * w]
            bux[rows, ci] = state[2 * w + 1]
    for k in range(nhalf):
        st[:, hs * k:hs * (k + 1)] = state[k]

    o = SSM_WIDTH
    put(0, proj(o) * (SB_HEAD_DIM ** -0.5 * LOG2_E))
    put(SB_WIDTH, proj(o + SB_WIDTH))
    put(2 * SB_WIDTH, proj(o + 2 * SB_WIDTH))
    cosf = cos_ref[...].reshape(m, LANES)
    sina = sina_ref[...].reshape(m, LANES)
    sinb = sinb_ref[...].reshape(m, LANES)
    src = o + 3 * SB_WIDTH
    dst = 3 * SB_WIDTH
    for scale in (DA_QK_DIM ** -0.5 * LOG2_E, 1.0):
        t4 = proj(src)
        for hh in range(DA_HEADS):
            t = t4[:, LANES * hh:LANES * (hh + 1)]
            r = (t * cosf + pltpu.roll(t, ROT_DIM // 2, 1) * sina
                 + pltpu.roll(t, LANES - ROT_DIM // 2, 1) * sinb)
            put(dst + LANES * hh, r * scale)
        src += DA_QK_WIDTH
        dst += DA_QK_WIDTH
    put(dst, proj(src))

    ys = []
    for w in range(SSM_WIN):
        xw = bux[:, wst * w:wst * (w + 1)].astype(BF16)
        yw = _dot(xw, cw_ref[w])
        yw = yw + d_ref[:, LANES * w:LANES * (w + 1)] * u_tb[w]
        ys.append(jax.nn.gelu(yw).astype(BF16))
    glu = _dot(jnp.concatenate(ys, axis=1), wglu_ref[...])
    for w in range(SSM_WIN):
        sl = slice(LANES * w, LANES * (w + 1))
        y_tb[w] = glu[:, sl] * jax.nn.sigmoid(glu[:, SSM_WIDTH + LANES * w:SSM_WIDTH + LANES * (w + 1)])
    for b in range(bsz):
        for w in range(SSM_WIN):
            yssm_ref[b, :, LANES * w:LANES * (w + 1)] = y_tb[w, pl.ds(b, ts, stride=bsz), :].astype(BF16)


def _inproj_ssm(x, gain, wa, rope, ssm, d_skip, w_glu, layer, *, ts):
    bsz, seq, _ = x.shape
    assert bsz == SUBLANES and seq % ts == 0
    cosf, sina, sinb = rope
    bw, lr, li, cw = ssm
    m = ts * bsz
    tile = lambda width: pl.BlockSpec((bsz, ts, width), lambda i: (0, i, 0))
    return pl.pallas_call(
        functools.partial(_inproj_ssm_kernel, ts=ts, bsz=bsz),
        grid=(seq // ts,),
        in_specs=[
            tile(D_MODEL),
            _const_spec((1, D_MODEL)),
            _layer_spec((D_MODEL, N_PROJ_A), layer),
            tile(LANES), tile(LANES), tile(LANES),
            _const_spec(bw.shape), _const_spec(lr.shape), _const_spec(li.shape), _const_spec(cw.shape),
            _const_spec((1, SSM_WIDTH)),
            _layer_spec((SSM_WIDTH, 2 * SSM_WIDTH), layer),
        ],
        out_specs=[tile(N_QKV), tile(SSM_WIDTH)],
        out_shape=[jax.ShapeDtypeStruct((bsz, seq, N_QKV), BF16),
                   jax.ShapeDtypeStruct((bsz, seq, SSM_WIDTH), BF16)],
        scratch_shapes=[
            pltpu.VMEM((SSM_WIN, m, LANES), F32),
            pltpu.VMEM((m, 2 * SSM_WIN * SSM_WIN_STATE), F32),
            pltpu.VMEM((bsz, 2 * SSM_WIN * SSM_WIN_STATE), F32),
            pltpu.VMEM((SSM_WIN, m, LANES), F32),
        ],
        compiler_params=_params("arbitrary"),
        name="inproj_ssm",
    )(x, gain.reshape(1, D_MODEL), wa, cosf, sina, sinb, bw, lr, li, cw,
      d_skip.reshape(1, SSM_WIDTH), w_glu)


def _sb_kernel(q_ref, k_ref, v_ref, o_ref, vt, *, tq):
    i = pl.program_id(2)
    tk = tq

    @pl.when(i == 0)
    def _():
        for c in range(vt.shape[0]):
            vt[c] = v_ref[0, tk * c:tk * (c + 1), :].astype(F32).T.astype(BF16)

    q2 = q_ref[0]
    lane = lax.broadcasted_iota(jnp.int32, (tq, LANES), 1)
    zero = jnp.zeros_like(q2)
    qs = (jnp.where(lane < SB_HEAD_DIM, q2, zero), jnp.where(lane >= SB_HEAD_DIM, q2, zero))
    key = lax.broadcasted_iota(jnp.int32, (tk, tq), 0)
    qry = lax.broadcasted_iota(jnp.int32, (tk, tq), 1)
    strict = key < qry
    tri = (lax.broadcasted_iota(jnp.int32, (tk, tk), 1)
           > lax.broadcasted_iota(jnp.int32, (tk, tk), 0)).astype(BF16)

    def blocks(js, masks, runs):
        kbs = [k_ref[0, pl.ds(pl.multiple_of(j * tk, tk), tk), :] for j in js]
        z = [[_dot_nt(kb, qs[a]) for a in range(2)] for kb in kbs]
        l1m, hi = [], []
        for n, masked in enumerate(masks):
            row = []
            for a in range(2):
                t = -(jnp.maximum(z[n][a], 0.0) + jnp.log2(1.0 + jnp.exp2(-jnp.abs(z[n][a]))))
                row.append(jnp.where(strict, t, 0.0) if masked else t)
            l1m.append(row)
            hi.append([t.astype(BF16) for t in row])
        cum = [[_dot(tri, hi[n][a]) for a in range(2)] for n in range(len(js))]
        w = []
        runs = list(runs)
        for n, masked in enumerate(masks):
            row = []
            for a in range(2):
                t = jnp.exp2(z[n][a] + l1m[n][a] + (cum[n][a] + runs[a]))
                row.append((jnp.where(strict, t, 0.0) if masked else t).astype(BF16))
                runs[a] = runs[a] + jnp.sum(l1m[n][a], axis=0, keepdims=True)
            w.append(row)
        pvs = [[_dot(vt[j], w[n][a]) for a in range(2)] for n, j in enumerate(js)]
        return pvs, runs

    zrun = jnp.zeros((1, tq), F32)
    ((acc0, acc1), (p0, p1)), (run0, run1) = blocks((i, jnp.maximum(i - 1, 0)), (True, False), (zrun, zrun))
    live = i > 0
    acc0 = acc0 + jnp.where(live, p0, 0.0)
    acc1 = acc1 + jnp.where(live, p1, 0.0)

    def cond(c):
        j, r0, r1, _, _ = c
        return jnp.logical_and(j >= 0, jnp.max(jnp.maximum(r0, r1)) > SB_LOG_CUTOFF)

    def body(c):
        j, r0, r1, a0, a1 = c
        ((p0, p1),), (r0, r1) = blocks((j,), (False,), (r0, r1))
        return j - 1, r0, r1, a0 + p0, a1 + p1

    _, _, _, acc0, acc1 = lax.while_loop(cond, body, (i - 2, run0, run1, acc0, acc1))
    dim = lax.broadcasted_iota(jnp.int32, (LANES, tq), 0)
    o_ref[0] = jnp.where(dim < SB_HEAD_DIM, acc0, acc1).T.astype(BF16)


def _sb_attention(qkv, *, tq):
    bsz, seq, _ = qkv.shape
    npair = SB_WIDTH // LANES
    return pl.pallas_call(
        functools.partial(_sb_kernel, tq=tq),
        grid=(bsz, npair, seq // tq),
        in_specs=[
            pl.BlockSpec((1, tq, LANES), lambda b, p, i: (b, i, SBQ_BLK + p)),
            pl.BlockSpec((1, seq, LANES), lambda b, p, i: (b, 0, SBK_BLK + p)),
            pl.BlockSpec((1, seq, LANES), lambda b, p, i: (b, 0, SBV_BLK + p)),
        ],
        out_specs=pl.BlockSpec((1, tq, LANES), lambda b, p, i: (b, i, p)),
        out_shape=jax.ShapeDtypeStruct((bsz, seq, SB_WIDTH), BF16),
        scratch_shapes=[pltpu.VMEM((seq // tq, LANES, tq), BF16)],
        compiler_params=_params("arbitrary", "arbitrary", "arbitrary"),
        name="sb_attn",
    )(qkv, qkv, qkv)


def _da_kernel(q_ref, k_ref, v_ref, lam_ref, g_ref, o_ref, vt, sa, sb, acc, *, tq, th, lam_init):
    i = pl.program_id(2)

    @pl.when(i == 0)
    def _():
        for c in range(vt.shape[0]):
            vt[c, :DA_V_DIM, :] = v_ref[0, th * c:th * (c + 1), :].astype(F32).T.astype(BF16)
            vt[c, DA_V_DIM:, :] = jnp.ones((DA_ONES_ROWS, th), BF16)

    q = q_ref[0]
    lane = lax.broadcasted_iota(jnp.int32, (tq, LANES), 1)
    zero = jnp.zeros_like(q)
    qs = (jnp.where(lane < DA_QK_DIM, q, zero), jnp.where(lane >= DA_QK_DIM, q, zero))

    def scores_to(buf, blk):
        ks = pl.multiple_of(blk * th, th)
        kb = k_ref[0, pl.ds(ks, th), :]
        for c in range(2):
            buf[c] = _dot_nt(kb, qs[c])

    def update(buf, blk, stats, masked):
        vtb = vt[blk]
        if masked:
            key = lax.broadcasted_iota(jnp.int32, (th, tq), 0) + blk * th
            qry = lax.broadcasted_iota(jnp.int32, (th, tq), 1) + i * tq
            causal = key <= qry
        out = []
        for c in range(2):
            m = stats[c]
            s = buf[c]
            if masked:
                s = jnp.where(causal, s, MASK_VALUE)
            mn = jnp.maximum(m, jnp.max(s, axis=0, keepdims=True))
            alpha = jnp.exp2(m - mn)
            p = jnp.exp2(s - mn).astype(BF16)
            acc[c] = alpha * acc[c] + _dot(vtb, p)
            out.append(mn)
        return tuple(out)

    def pair(j, stats, masked):
        scores_to(sb, 2 * j + 1)
        stats = update(sa, 2 * j, stats, masked)
        if not masked:
            scores_to(sa, 2 * j + 2)
        return update(sb, 2 * j + 1, stats, masked)

    acc[...] = jnp.zeros_like(acc)
    scores_to(sa, 0)
    init = tuple(jnp.full((1, tq), MASK_VALUE, F32) for _ in range(2))
    stats = lax.fori_loop(0, i, lambda j, st: pair(j, st, False), init)
    pair(i, stats, True)
    l1 = acc[0, DA_V_DIM:DA_V_DIM + 1, :]
    l2 = acc[1, DA_V_DIM:DA_V_DIM + 1, :]

    lv = lam_ref[...]
    lam = (jnp.exp(jnp.sum(lv[0:1] * lv[1:2], axis=1, keepdims=True))
           - jnp.exp(jnp.sum(lv[2:3] * lv[3:4], axis=1, keepdims=True)) + lam_init)
    ot = acc[0, :DA_V_DIM, :] * (1.0 / l1) - acc[1, :DA_V_DIM, :] * (lam / l2)
    ms = jnp.mean(ot * ot, axis=0, keepdims=True)
    ot = ot * (lax.rsqrt(ms + RMS_EPS) * (1.0 - lam_init)) * g_ref[...]
    o_ref[0] = ot.T.astype(BF16)


def _da_attention(qkv, diff_lambda, diff_subln, lam_init, *, tq):
    bsz, seq, _ = qkv.shape
    th = tq // 2
    assert seq % tq == 0
    return pl.pallas_call(
        functools.partial(_da_kernel, tq=tq, th=th, lam_init=lam_init),
        grid=(bsz, DA_HEADS, seq // tq),
        in_specs=[
            pl.BlockSpec((1, tq, LANES), lambda b, h, i: (b, i, DAQ_BLK + h)),
            pl.BlockSpec((1, seq, LANES), lambda b, h, i: (b, 0, DAK_BLK + h)),
            pl.BlockSpec((1, seq, LANES), lambda b, h, i: (b, 0, DAV_BLK + h)),
            pl.BlockSpec((4, DA_QK_DIM), lambda b, h, i: (0, 0)),
            pl.BlockSpec((DA_V_DIM, 1), lambda b, h, i: (0, 0)),
        ],
        out_specs=pl.BlockSpec((1, tq, LANES), lambda b, h, i: (b, i, h)),
        out_shape=jax.ShapeDtypeStruct((bsz, seq, DA_WIDTH), BF16),
        scratch_shapes=[pltpu.VMEM((seq // th, DA_V_DIM + DA_ONES_ROWS, th), BF16),
                        pltpu.VMEM((2, th, tq), F32), pltpu.VMEM((2, th, tq), F32),
                        pltpu.VMEM((2, DA_V_DIM + DA_ONES_ROWS, tq), F32)],
        compiler_params=_params("arbitrary", "arbitrary", "arbitrary"),
        name="da_attn",
    )(qkv, qkv, qkv, diff_lambda, diff_subln.reshape(DA_V_DIM, 1))


def _merge_cross_mlp_kernel(x_ref, yssm_ref, ysb_ref, yda_ref, gmix_ref, wg_ref, wbr_ref, wout_ref,
                            gcross_ref, wxq_ref, kv_ref, wxo_ref, gmlp_ref, wup_ref, wdown_ref, gfin_ref,
                            o_ref, *, final, chunk):
    x = x_ref[0]
    h = _rms(x, gmix_ref[...]).astype(BF16)
    merged = None
    for n, y_ref in enumerate((yssm_ref, ysb_ref, yda_ref)):
        gate = jax.nn.sigmoid(_dot(h, wg_ref[:, D_MODEL * n:D_MODEL * (n + 1)]))
        term = gate * _dot(y_ref[0], wbr_ref[n])
        merged = term if merged is None else merged + term
    x1 = x + _dot(merged.astype(BF16), wout_ref[...])

    hx = _rms(x1, gcross_ref[...]).astype(BF16)
    xq = _dot(hx, wxq_ref[...]).astype(BF16)
    kv = kv_ref[0]
    heads = []
    for hh in range(XA_HEADS):
        sl = slice(XA_HEAD_DIM * hh, XA_HEAD_DIM * (hh + 1))
        s = _dot_nt(xq[:, sl], kv[:, sl]) * XA_HEAD_DIM ** -0.5
        p = jnp.exp(s - jnp.max(s, axis=1, keepdims=True))
        vh = kv[:, XA_WIDTH + XA_HEAD_DIM * hh:XA_WIDTH + XA_HEAD_DIM * (hh + 1)]
        heads.append((_dot(p.astype(BF16), vh) / jnp.sum(p, axis=1, keepdims=True)).astype(BF16))
    x2 = x1 + _dot(jnp.concatenate(heads, axis=1), wxo_ref[...])

    hm = _rms(x2, gmlp_ref[...]).astype(BF16)
    acc = x2
    for c in range(MLP_HIDDEN // chunk):
        up = _dot(hm, wup_ref[:, chunk * c:chunk * (c + 1)])
        act = jnp.square(jnp.maximum(up, 0.0)).astype(BF16)
        acc = acc + _dot(act, wdown_ref[chunk * c:chunk * (c + 1), :])
    if final:
        acc = _rms(acc, gfin_ref[...])
    o_ref[0] = acc


def _merge_cross_mlp(x, yssm, ysb, yda, gmix, wg, wbr, wout, gcross, wxq, kv, wxo, gmlp, wup, wdown, gfin,
                     layer, *, final, tm):
    bsz, seq, _ = x.shape
    tile = lambda width: pl.BlockSpec((1, tm, width), lambda b, i: (b, i, 0))
    gain = lambda g: g.reshape(1, D_MODEL)
    return pl.pallas_call(
        functools.partial(_merge_cross_mlp_kernel, final=final, chunk=1024),
        grid=(bsz, seq // tm),
        in_specs=[
            tile(D_MODEL), tile(BRANCH_WIDTH), tile(BRANCH_WIDTH), tile(BRANCH_WIDTH),
            _const_spec((1, D_MODEL)),
            _layer_spec((D_MODEL, N_BRANCH * D_MODEL), layer),
            _layer_spec((N_BRANCH, BRANCH_WIDTH, D_MODEL), layer),
            _layer_spec((D_MODEL, D_MODEL), layer),
            _const_spec((1, D_MODEL)),
            _layer_spec((D_MODEL, XA_WIDTH), layer),
            pl.BlockSpec((None, 1, MEM_LEN, 2 * XA_WIDTH), lambda b, i: (layer, b, 0, 0)),
            _layer_spec((XA_WIDTH, D_MODEL), layer),
            _const_spec((1, D_MODEL)),
            _layer_spec((D_MODEL, MLP_HIDDEN), layer),
            _layer_spec((MLP_HIDDEN, D_MODEL), layer),
            _const_spec((1, D_MODEL)),
        ],
        out_specs=tile(D_MODEL),
        out_shape=jax.ShapeDtypeStruct((bsz, seq, D_MODEL), F32),
        compiler_params=_params("parallel", "arbitrary"),
        name="merge_cross_mlp",
    )(x, yssm, ysb, yda, gain(gmix), wg, wbr, wout, gain(gcross), wxq, kv, wxo, gain(gmlp), wup, wdown, gain(gfin))


def _ssm_tables(lam_re, lam_im, log_dt, b_re, b_im, c_re, c_im, bsz):
    dt = jnp.exp(log_dt)[:, None]
    mag = jnp.exp(lam_re * dt)
    lbr = mag * jnp.cos(lam_im * dt)
    lbi = mag * jnp.sin(lam_im * dt)
    den = lam_re * lam_re + lam_im * lam_im
    fr = ((lbr - 1.0) * lam_re + lbi * lam_im) / den
    fi = (lbi * lam_re - (lbr - 1.0) * lam_im) / den
    bbr = fr[..., None] * b_re - fi[..., None] * b_im
    bbi = fr[..., None] * b_im + fi[..., None] * b_re
    gl = SSM_GROUPS // SSM_WIN
    eye = jnp.eye(gl, dtype=F32)
    shp = (SSM_WIN, gl, SSM_STATE, SSM_GROUP)
    to_b = lambda t: jnp.einsum("wgpc,gh->wgchp", t.reshape(shp), eye)
    bw = jnp.stack([to_b(bbr), to_b(bbi)], axis=3).reshape(SSM_WIN, LANES, 2 * SSM_WIN_STATE)
    shc = (SSM_WIN, gl, SSM_GROUP, SSM_STATE)
    to_c = lambda t: jnp.einsum("wgcp,gh->wgphc", t.reshape(shc), eye)
    cw = jnp.stack([to_c(c_re), -to_c(c_im)], axis=1).reshape(SSM_WIN, 2 * SSM_WIN_STATE, LANES)
    lr = jnp.broadcast_to(lbr.reshape(1, -1), (bsz, SSM_GROUPS * SSM_STATE))
    li = jnp.broadcast_to(lbi.reshape(1, -1), (bsz, SSM_GROUPS * SSM_STATE))
    return bw.astype(BF16), lr, li, cw.astype(BF16)


def _rope_tables(positions):
    half = ROT_DIM // 2
    inv_freq = ROPE_THETA ** (-jnp.arange(0, ROT_DIM, 2, dtype=F32) / ROT_DIM)
    ang = positions.astype(F32)[..., None] * inv_freq
    lane = np.arange(LANES) % DA_QK_DIM
    hit = (lane[None, :] % half) == np.arange(half)[:, None]
    spread = lambda t, sel: jnp.dot(t, jnp.asarray(sel, F32), precision=lax.Precision.HIGHEST)
    cosf = spread(jnp.cos(ang), hit & (lane < ROT_DIM)) + jnp.asarray(lane >= ROT_DIM, F32)
    sina = spread(jnp.sin(ang), hit & (lane >= half) & (lane < ROT_DIM))
    sinb = spread(-jnp.sin(ang), hit & (lane < half))
    return cosf, sina, sinb


def kernel(x, mem, positions, norm_mix, w_in, ssm_lam_re, ssm_lam_im, ssm_log_dt, ssm_b_re, ssm_b_im, ssm_c_re, ssm_c_im, ssm_d, ssm_w_glu, diff_lambda, diff_subln, w_branch, w_out, norm_cross, norm_mem, w_xq, w_xkv, w_xo, norm_mlp, w_up, w_down, norm_final):
    bsz, seq, _ = x.shape
    depth = w_in.shape[0]
    ts = min(64, seq)
    sb_tq = min(256, seq)
    da_tq = min(512, seq)
    tm = min(512, seq)

    rope = _rope_tables(positions)
    kv_all = _memkv(mem, norm_mem, w_xkv.astype(BF16))
    wa = w_in[:, :, :N_PROJ_A].astype(BF16)
    wg = w_in[:, :, N_PROJ_A:].astype(BF16)
    wglu, wbr, wout = ssm_w_glu.astype(BF16), w_branch.astype(BF16), w_out.astype(BF16)
    wxq, wxo, wup, wdown = w_xq.astype(BF16), w_xo.astype(BF16), w_up.astype(BF16), w_down.astype(BF16)

    for l in range(depth):
        ssm = _ssm_tables(ssm_lam_re[l], ssm_lam_im[l], ssm_log_dt[l], ssm_b_re[l], ssm_b_im[l],
                          ssm_c_re[l], ssm_c_im[l], bsz)
        qkv, yssm = _inproj_ssm(x, norm_mix[l], wa, rope, ssm, ssm_d[l], wglu, l, ts=ts)
        ysb = _sb_attention(qkv, tq=sb_tq)
        lam_init = 0.8 - 0.6 * math.exp(-0.3 * l)
        yda = _da_attention(qkv, diff_lambda[l], diff_subln[l], lam_init, tq=da_tq)
        x = _merge_cross_mlp(x, yssm, ysb, yda, norm_mix[l], wg, wbr, wout, norm_cross[l], wxq, kv_all, wxo,
                             norm_mlp[l], wup, wdown, norm_final, l, final=(l == depth - 1), tm=tm)
    return x
```

```python
import functools
import math

import jax
import jax.numpy as jnp
import numpy as np
from jax import lax
from jax.experimental import pallas as pl
from jax.experimental.pallas import tpu as pltpu

F32 = jnp.float32
BF16 = jnp.bfloat16

D_MODEL = 1024
MEM_LEN = 256
RMS_EPS = 1e-6
SSM_WIDTH = 512
SSM_GROUP = 16
SSM_GROUPS = 32
SSM_STATE = 64
SB_HEADS = 8
SB_HEAD_DIM = 64
SB_WIDTH = 512
DA_HEADS = 4
DA_QK_DIM = 64
DA_V_DIM = 128
DA_QK_WIDTH = 512
DA_WIDTH = 512
ROT_DIM = 16
ROPE_THETA = 500000.0
N_BRANCH = 3
BRANCH_WIDTH = 512
XA_HEADS = 4
XA_HEAD_DIM = 128
XA_WIDTH = 512
MLP_HIDDEN = 4096

LANES = 128
SUBLANES = 8
VMEM_LIMIT_BYTES = 58 * 1024 * 1024

N_PROJ_A = SSM_WIDTH + 3 * SB_WIDTH + 2 * DA_QK_WIDTH + DA_WIDTH
N_QKV = N_PROJ_A - SSM_WIDTH
SBQ_BLK, SBK_BLK, SBV_BLK = 0, 4, 8
DAQ_BLK, DAK_BLK, DAV_BLK = 12, 16, 20

SSM_WIN = SSM_WIDTH // LANES
SSM_WIN_STATE = (SSM_GROUPS // SSM_WIN) * SSM_STATE

LOG2_E = 1.4426950408889634
SB_LOG2_REACH = 150.0
MASK_VALUE = -1e30
DA_ONES_ROWS = 16


def _rms(x, gain):
    ms = jnp.mean(x * x, axis=-1, keepdims=True)
    return x * lax.rsqrt(ms + RMS_EPS) * gain


def _dot(a, b):
    return jnp.dot(a, b, preferred_element_type=F32)


def _dot_nt(a, b):
    return lax.dot_general(a, b, (((1,), (1,)), ((), ())), preferred_element_type=F32)


def _params(*sem):
    return pltpu.CompilerParams(dimension_semantics=sem, vmem_limit_bytes=VMEM_LIMIT_BYTES)


def _const_spec(shape):
    nd = len(shape)
    return pl.BlockSpec(shape, lambda *_: (0,) * nd, pipeline_mode=pl.Buffered(1))


def _layer_spec(shape, layer):
    nd = len(shape)
    return pl.BlockSpec((None,) + tuple(shape), lambda *_: (layer,) + (0,) * nd, pipeline_mode=pl.Buffered(1))


def _memkv_kernel(mem_ref, g_ref, w_ref, o_ref):
    h = _rms(mem_ref[0], g_ref[0]).astype(BF16)
    o_ref[0, 0] = _dot(h, w_ref[0]).astype(BF16)


def _memkv(mem, norm_mem, w_xkv_bf):
    depth = w_xkv_bf.shape[0]
    bsz = mem.shape[0]
    return pl.pallas_call(
        _memkv_kernel,
        grid=(depth, bsz),
        in_specs=[
            pl.BlockSpec((1, MEM_LEN, D_MODEL), lambda l, b: (b, 0, 0)),
            pl.BlockSpec((1, 1, D_MODEL), lambda l, b: (l, 0, 0)),
            pl.BlockSpec((1, D_MODEL, 2 * XA_WIDTH), lambda l, b: (l, 0, 0)),
        ],
        out_specs=pl.BlockSpec((1, 1, MEM_LEN, 2 * XA_WIDTH), lambda l, b: (l, b, 0, 0)),
        out_shape=jax.ShapeDtypeStruct((depth, bsz, MEM_LEN, 2 * XA_WIDTH), BF16),
        compiler_params=_params("arbitrary", "arbitrary"),
        name="memkv",
    )(mem, norm_mem.reshape(depth, 1, D_MODEL), w_xkv_bf)


def _inproj_ssm_kernel(x_ref, g_ref, wa_ref, cos_ref, sina_ref, sinb_ref, bw_ref, lr_ref, li_ref,
                       cw_ref, d_ref, wglu_ref, qkv_ref, yssm_ref, u_tb, bux, st, y_tb, *, ts, bsz):
    @pl.when(pl.program_id(0) == 0)
    def _():
        st[...] = jnp.zeros_like(st)

    m = ts * bsz
    h = _rms(x_ref[...].reshape(m, D_MODEL), g_ref[...]).astype(BF16)

    def proj(c0, width=SSM_WIDTH):
        return _dot(h, wa_ref[:, c0:c0 + width])

    def put(dst, val):
        qkv_ref[:, :, dst:dst + val.shape[1]] = val.astype(BF16).reshape(bsz, ts, val.shape[1])

    u = proj(0)
    for b in range(bsz):
        for w in range(SSM_WIN):
            u_tb[w, pl.ds(b, ts, stride=bsz), :] = u[ts * b:ts * (b + 1), LANES * w:LANES * (w + 1)]

    wst = 2 * SSM_WIN_STATE
    for w in range(SSM_WIN):
        bux[:, wst * w:wst * (w + 1)] = _dot(u_tb[w].astype(BF16), bw_ref[w])

    nhalf = 2 * SSM_WIN
    hs = SSM_WIN_STATE
    state = [st[:, hs * k:hs * (k + 1)] for k in range(nhalf)]
    for t in range(ts):
        rows = slice(bsz * t, bsz * (t + 1))
        for w in range(SSM_WIN):
            xr, xi = state[2 * w], state[2 * w + 1]
            lr = lr_ref[:, hs * w:hs * (w + 1)]
            li = li_ref[:, hs * w:hs * (w + 1)]
            cr = slice(hs * 2 * w, hs * (2 * w + 1))
            ci = slice(hs * (2 * w + 1), hs * (2 * w + 2))
            state[2 * w] = lr * xr - li * xi + bux[rows, cr]
            state[2 * w + 1] = lr * xi + li * xr + bux[rows, ci]
            bux[rows, cr] = state[2 * w]
            bux[rows, ci] = state[2 * w + 1]
    for k in range(nhalf):
        st[:, hs * k:hs * (k + 1)] = state[k]

    o = SSM_WIDTH
    put(0, proj(o) * (SB_HEAD_DIM ** -0.5 * LOG2_E))
    put(SB_WIDTH, proj(o + SB_WIDTH))
    put(2 * SB_WIDTH, proj(o + 2 * SB_WIDTH))
    cosf = cos_ref[...].reshape(m, LANES)
    sina = sina_ref[...].reshape(m, LANES)
    sinb = sinb_ref[...].reshape(m, LANES)
    src = o + 3 * SB_WIDTH
    dst = 3 * SB_WIDTH
    for scale in (DA_QK_DIM ** -0.5 * LOG2_E, 1.0):
        t4 = proj(src)
        for hh in range(DA_HEADS):
            t = t4[:, LANES * hh:LANES * (hh + 1)]
            r = (t * cosf + pltpu.roll(t, ROT_DIM // 2, 1) * sina
                 + pltpu.roll(t, LANES - ROT_DIM // 2, 1) * sinb)
            put(dst + LANES * hh, r * scale)
        src += DA_QK_WIDTH
        dst += DA_QK_WIDTH
    put(dst, proj(src))

    ys = []
    for w in range(SSM_WIN):
        xw = bux[:, wst * w:wst * (w + 1)].astype(BF16)
        yw = _dot(xw, cw_ref[w])
        yw = yw + d_ref[:, LANES * w:LANES * (w + 1)] * u_tb[w]
        ys.append(jax.nn.gelu(yw).astype(BF16))
    glu = _dot(jnp.concatenate(ys, axis=1), wglu_ref[...])
    for w in range(SSM_WIN):
        sl = slice(LANES * w, LANES * (w + 1))
        y_tb[w] = glu[:, sl] * jax.nn.sigmoid(glu[:, SSM_WIDTH + LANES * w:SSM_WIDTH + LANES * (w + 1)])
    for b in range(bsz):
        for w in range(SSM_WIN):
            yssm_ref[b, :, LANES * w:LANES * (w + 1)] = y_tb[w, pl.ds(b, ts, stride=bsz), :].astype(BF16)


def _inproj_ssm(x, gain, wa, rope, ssm, d_skip, w_glu, layer, *, ts):
    bsz, seq, _ = x.shape
    assert bsz == SUBLANES and seq % ts == 0
    cosf, sina, sinb = rope
    bw, lr, li, cw = ssm
    m = ts * bsz
    tile = lambda width: pl.BlockSpec((bsz, ts, width), lambda i: (0, i, 0))
    return pl.pallas_call(
        functools.partial(_inproj_ssm_kernel, ts=ts, bsz=bsz),
        grid=(seq // ts,),
        in_specs=[
            tile(D_MODEL),
            _const_spec((1, D_MODEL)),
            _layer_spec((D_MODEL, N_PROJ_A), layer),
            tile(LANES), tile(LANES), tile(LANES),
            _const_spec(bw.shape), _const_spec(lr.shape), _const_spec(li.shape), _const_spec(cw.shape),
            _const_spec((1, SSM_WIDTH)),
            _layer_spec((SSM_WIDTH, 2 * SSM_WIDTH), layer),
        ],
        out_specs=[tile(N_QKV), tile(SSM_WIDTH)],
        out_shape=[jax.ShapeDtypeStruct((bsz, seq, N_QKV), BF16),
                   jax.ShapeDtypeStruct((bsz, seq, SSM_WIDTH), BF16)],
        scratch_shapes=[
            pltpu.VMEM((SSM_WIN, m, LANES), F32),
            pltpu.VMEM((m, 2 * SSM_WIN * SSM_WIN_STATE), F32),
            pltpu.VMEM((bsz, 2 * SSM_WIN * SSM_WIN_STATE), F32),
            pltpu.VMEM((SSM_WIN, m, LANES), F32),
        ],
        compiler_params=_params("arbitrary"),
        name="inproj_ssm",
    )(x, gain.reshape(1, D_MODEL), wa, cosf, sina, sinb, bw, lr, li, cw,
      d_skip.reshape(1, SSM_WIDTH), w_glu)


def _sb_kernel(q_ref, k_ref, v_ref, o_ref, vt, *, tq, npair):
    i = pl.program_id(2)
    tk = tq
    nhead = 2 * npair
    blk = lambda p: slice(LANES * p, LANES * (p + 1))

    @pl.when(i == 0)
    def _():
        for p in range(npair):
            for c in range(vt.shape[1]):
                vt[p, c] = v_ref[0, tk * c:tk * (c + 1), blk(p)].astype(F32).T.astype(BF16)

    lane = lax.broadcasted_iota(jnp.int32, (tq, LANES), 1)
    qs = []
    for p in range(npair):
        q2 = q_ref[0, :, blk(p)]
        zero = jnp.zeros_like(q2)
        qs += [jnp.where(lane < SB_HEAD_DIM, q2, zero), jnp.where(lane >= SB_HEAD_DIM, q2, zero)]
    key = lax.broadcasted_iota(jnp.int32, (tk, tq), 0)
    qry = lax.broadcasted_iota(jnp.int32, (tk, tq), 1)
    strict = key < qry
    tri = (lax.broadcasted_iota(jnp.int32, (tk, tk), 1)
           > lax.broadcasted_iota(jnp.int32, (tk, tk), 0)).astype(BF16)

    def blocks(js, masks, runs):
        rows = [pl.ds(pl.multiple_of(j * tk, tk), tk) for j in js]
        z = [[_dot_nt(k_ref[0, r, blk(a // 2)], qs[a]) for a in range(nhead)] for r in rows]
        sp, hi = [], []
        for n, masked in enumerate(masks):
            if masked:
                z[n] = [jnp.where(strict, t, MASK_VALUE) for t in z[n]]
            row = []
            for a in range(nhead):
                t = z[n][a]
                row.append(jnp.maximum(t, 0.0) + jnp.log2(1.0 + jnp.exp2(-jnp.abs(t))))
            sp.append(row)
            hi.append([t.astype(BF16) for t in row])
        cum = [[_dot(tri, hi[n][a]) for a in range(nhead)] for n in range(len(js))]
        w = []
        runs = list(runs)
        for n in range(len(js)):
            row = []
            for a in range(nhead):
                row.append(jnp.exp2(z[n][a] - sp[n][a] - (cum[n][a] + runs[a])).astype(BF16))
                runs[a] = runs[a] + jnp.sum(sp[n][a], axis=0, keepdims=True)
            w.append(row)
        pvs = [[_dot(vt[a // 2, j], w[n][a]) for a in range(nhead)] for n, j in enumerate(js)]
        return pvs, runs

    zrun = jnp.zeros((1, tq), F32)
    (accs, prev), runs = blocks((i, jnp.maximum(i - 1, 0)), (True, False), [zrun] * nhead)
    live = i > 0
    accs = [a + jnp.where(live, p, 0.0) for a, p in zip(accs, prev)]

    def cond(c):
        j, runs, _ = c
        return jnp.logical_and(j >= 0, jnp.min(functools.reduce(jnp.minimum, runs)) < SB_LOG2_REACH)

    def body(c):
        j, runs, accs = c
        (pvs,), runs = blocks((j,), (False,), runs)
        return j - 1, tuple(runs), tuple(a + p for a, p in zip(accs, pvs))

    _, _, accs = lax.while_loop(cond, body, (i - 2, tuple(runs), tuple(accs)))
    dim = lax.broadcasted_iota(jnp.int32, (LANES, tq), 0)
    for p in range(npair):
        o_ref[0, :, blk(p)] = jnp.where(dim < SB_HEAD_DIM, accs[2 * p], accs[2 * p + 1]).T.astype(BF16)


def _sb_attention(qkv, *, tq, npair):
    bsz, seq, _ = qkv.shape
    width = LANES * npair
    ngroup = SB_WIDTH // width
    col = lambda first: (lambda b, g, i: (b, 0, first * LANES // width + g))
    return pl.pallas_call(
        functools.partial(_sb_kernel, tq=tq, npair=npair),
        grid=(bsz, ngroup, seq // tq),
        in_specs=[
            pl.BlockSpec((1, tq, width), lambda b, g, i: (b, i, SBQ_BLK * LANES // width + g)),
            pl.BlockSpec((1, seq, width), col(SBK_BLK)),
            pl.BlockSpec((1, seq, width), col(SBV_BLK)),
        ],
        out_specs=pl.BlockSpec((1, tq, width), lambda b, g, i: (b, i, g)),
        out_shape=jax.ShapeDtypeStruct((bsz, seq, SB_WIDTH), BF16),
        scratch_shapes=[pltpu.VMEM((npair, seq // tq, LANES, tq), BF16)],
        compiler_params=_params("arbitrary", "arbitrary", "arbitrary"),
        name="sb_attn",
    )(qkv, qkv, qkv)


def _da_kernel(q_ref, k_ref, v_ref, lam_ref, g_ref, o_ref, vt, sa, sb, acc, *, tq, th, nhd, lam_init):
    i = pl.program_id(2)
    nchain = 2 * nhd
    blk = lambda hh: slice(LANES * hh, LANES * (hh + 1))

    @pl.when(i == 0)
    def _():
        for hh in range(nhd):
            for c in range(vt.shape[1]):
                vt[hh, c, :DA_V_DIM, :] = v_ref[0, th * c:th * (c + 1), blk(hh)].astype(F32).T.astype(BF16)
                vt[hh, c, DA_V_DIM:, :] = jnp.ones((DA_ONES_ROWS, th), BF16)

    lane = lax.broadcasted_iota(jnp.int32, (tq, LANES), 1)
    qs = []
    for hh in range(nhd):
        q = q_ref[0, :, blk(hh)]
        zero = jnp.zeros_like(q)
        qs += [jnp.where(lane < DA_QK_DIM, q, zero), jnp.where(lane >= DA_QK_DIM, q, zero)]

    def scores_to(buf, kblk):
        rows = pl.ds(pl.multiple_of(kblk * th, th), th)
        for x in range(nchain):
            buf[x] = _dot_nt(k_ref[0, rows, blk(x // 2)], qs[x])

    def update(buf, kblk, stats, masked):
        if masked:
            key = lax.broadcasted_iota(jnp.int32, (th, tq), 0) + kblk * th
            qry = lax.broadcasted_iota(jnp.int32, (th, tq), 1) + i * tq
            causal = key <= qry
        new, ps = [], []
        for x in range(nchain):
            s = buf[x]
            if masked:
                s = jnp.where(causal, s, MASK_VALUE)
            mn = jnp.maximum(stats[x], jnp.max(s, axis=0, keepdims=True))
            ps.append(jnp.exp2(s - mn).astype(BF16))
            new.append(mn)
        for x in range(nchain):
            acc[x] = jnp.exp2(stats[x] - new[x]) * acc[x] + _dot(vt[x // 2, kblk], ps[x])
        return tuple(new)

    def pair(j, stats, masked):
        scores_to(sb, 2 * j + 1)
        stats = update(sa, 2 * j, stats, masked)
        if not masked:
            scores_to(sa, 2 * j + 2)
        return update(sb, 2 * j + 1, stats, masked)

    acc[...] = jnp.zeros_like(acc)
    scores_to(sa, 0)
    init = tuple(jnp.full((1, tq), MASK_VALUE, F32) for _ in range(nchain))
    stats = lax.fori_loop(0, i, lambda j, st: pair(j, st, False), init)
    pair(i, stats, True)

    lv = lam_ref[...]
    lam = (jnp.exp(jnp.sum(lv[0:1] * lv[1:2], axis=1, keepdims=True))
           - jnp.exp(jnp.sum(lv[2:3] * lv[3:4], axis=1, keepdims=True)) + lam_init)
    for hh in range(nhd):
        l1 = acc[2 * hh, DA_V_DIM:DA_V_DIM + 1, :]
        l2 = acc[2 * hh + 1, DA_V_DIM:DA_V_DIM + 1, :]
        ot = acc[2 * hh, :DA_V_DIM, :] * (1.0 / l1) - acc[2 * hh + 1, :DA_V_DIM, :] * (lam / l2)
        ms = jnp.mean(ot * ot, axis=0, keepdims=True)
        ot = ot * (lax.rsqrt(ms + RMS_EPS) * (1.0 - lam_init)) * g_ref[...]
        o_ref[0, :, blk(hh)] = ot.T.astype(BF16)


def _da_attention(qkv, diff_lambda, diff_subln, lam_init, *, tq, nhd):
    bsz, seq, _ = qkv.shape
    th = tq // 2
    width = LANES * nhd
    assert seq % tq == 0 and DA_HEADS % nhd == 0
    col = lambda first: (lambda b, g, i: (b, 0, first * LANES // width + g))
    return pl.pallas_call(
        functools.partial(_da_kernel, tq=tq, th=th, nhd=nhd, lam_init=lam_init),
        grid=(bsz, DA_HEADS // nhd, seq // tq),
        in_specs=[
            pl.BlockSpec((1, tq, width), lambda b, g, i: (b, i, DAQ_BLK * LANES // width + g)),
            pl.BlockSpec((1, seq, width), col(DAK_BLK)),
            pl.BlockSpec((1, seq, width), col(DAV_BLK)),
            pl.BlockSpec((4, DA_QK_DIM), lambda b, g, i: (0, 0)),
            pl.BlockSpec((DA_V_DIM, 1), lambda b, g, i: (0, 0)),
        ],
        out_specs=pl.BlockSpec((1, tq, width), lambda b, g, i: (b, i, g)),
        out_shape=jax.ShapeDtypeStruct((bsz, seq, DA_WIDTH), BF16),
        scratch_shapes=[pltpu.VMEM((nhd, seq // th, DA_V_DIM + DA_ONES_ROWS, th), BF16),
                        pltpu.VMEM((2 * nhd, th, tq), F32), pltpu.VMEM((2 * nhd, th, tq), F32),
                        pltpu.VMEM((2 * nhd, DA_V_DIM + DA_ONES_ROWS, tq), F32)],
        compiler_params=_params("arbitrary", "arbitrary", "arbitrary"),
        name="da_attn",
    )(qkv, qkv, qkv, diff_lambda, diff_subln.reshape(DA_V_DIM, 1))


def _merge_cross_mlp_kernel(x_ref, yssm_ref, ysb_ref, yda_ref, gmix_ref, wg_ref, wbr_ref, wout_ref,
                            gcross_ref, wxq_ref, kv_ref, wxo_ref, gmlp_ref, wup_ref, wdown_ref, gfin_ref,
                            o_ref, *, final, chunk):
    x = x_ref[0]
    h = _rms(x, gmix_ref[...]).astype(BF16)
    merged = None
    for n, y_ref in enumerate((yssm_ref, ysb_ref, yda_ref)):
        gate = jax.nn.sigmoid(_dot(h, wg_ref[:, D_MODEL * n:D_MODEL * (n + 1)]))
        term = gate * _dot(y_ref[0], wbr_ref[n])
        merged = term if merged is None else merged + term
    x1 = x + _dot(merged.astype(BF16), wout_ref[...])

    hx = _rms(x1, gcross_ref[...]).astype(BF16)
    xq = _dot(hx, wxq_ref[...]).astype(BF16)
    kv = kv_ref[0]
    heads = []
    for hh in range(XA_HEADS):
        sl = slice(XA_HEAD_DIM * hh, XA_HEAD_DIM * (hh + 1))
        s = _dot_nt(xq[:, sl], kv[:, sl]) * XA_HEAD_DIM ** -0.5
        p = jnp.exp(s - jnp.max(s, axis=1, keepdims=True))
        vh = kv[:, XA_WIDTH + XA_HEAD_DIM * hh:XA_WIDTH + XA_HEAD_DIM * (hh + 1)]
        heads.append((_dot(p.astype(BF16), vh) / jnp.sum(p, axis=1, keepdims=True)).astype(BF16))
    x2 = x1 + _dot(jnp.concatenate(heads, axis=1), wxo_ref[...])

    hm = _rms(x2, gmlp_ref[...]).astype(BF16)
    acc = x2
    for c in range(MLP_HIDDEN // chunk):
        up = _dot(hm, wup_ref[:, chunk * c:chunk * (c + 1)])
        act = jnp.square(jnp.maximum(up, 0.0)).astype(BF16)
        acc = acc + _dot(act, wdown_ref[chunk * c:chunk * (c + 1), :])
    if final:
        acc = _rms(acc, gfin_ref[...])
    o_ref[0] = acc


def _merge_cross_mlp(x, yssm, ysb, yda, gmix, wg, wbr, wout, gcross, wxq, kv, wxo, gmlp, wup, wdown, gfin,
                     layer, *, final, tm):
    bsz, seq, _ = x.shape
    tile = lambda width: pl.BlockSpec((1, tm, width), lambda b, i: (b, i, 0))
    gain = lambda g: g.reshape(1, D_MODEL)
    return pl.pallas_call(
        functools.partial(_merge_cross_mlp_kernel, final=final, chunk=1024),
        grid=(bsz, seq // tm),
        in_specs=[
            tile(D_MODEL), tile(BRANCH_WIDTH), tile(BRANCH_WIDTH), tile(BRANCH_WIDTH),
            _const_spec((1, D_MODEL)),
            _layer_spec((D_MODEL, N_BRANCH * D_MODEL), layer),
            _layer_spec((N_BRANCH, BRANCH_WIDTH, D_MODEL), layer),
            _layer_spec((D_MODEL, D_MODEL), layer),
            _const_spec((1, D_MODEL)),
            _layer_spec((D_MODEL, XA_WIDTH), layer),
            pl.BlockSpec((None, 1, MEM_LEN, 2 * XA_WIDTH), lambda b, i: (layer, b, 0, 0)),
            _layer_spec((XA_WIDTH, D_MODEL), layer),
            _const_spec((1, D_MODEL)),
            _layer_spec((D_MODEL, MLP_HIDDEN), layer),
            _layer_spec((MLP_HIDDEN, D_MODEL), layer),
            _const_spec((1, D_MODEL)),
        ],
        out_specs=tile(D_MODEL),
        out_shape=jax.ShapeDtypeStruct((bsz, seq, D_MODEL), F32),
        compiler_params=_params("parallel", "arbitrary"),
        name="merge_cross_mlp",
    )(x, yssm, ysb, yda, gain(gmix), wg, wbr, wout, gain(gcross), wxq, kv, wxo, gain(gmlp), wup, wdown, gain(gfin))


def _ssm_tables(lam_re, lam_im, log_dt, b_re, b_im, c_re, c_im, bsz):
    dt = jnp.exp(log_dt)[:, None]
    mag = jnp.exp(lam_re * dt)
    lbr = mag * jnp.cos(lam_im * dt)
    lbi = mag * jnp.sin(lam_im * dt)
    den = lam_re * lam_re + lam_im * lam_im
    fr = ((lbr - 1.0) * lam_re + lbi * lam_im) / den
    fi = (lbi * lam_re - (lbr - 1.0) * lam_im) / den
    bbr = fr[..., None] * b_re - fi[..., None] * b_im
    bbi = fr[..., None] * b_im + fi[..., None] * b_re
    gl = SSM_GROUPS // SSM_WIN
    eye = jnp.eye(gl, dtype=F32)
    shp = (SSM_WIN, gl, SSM_STATE, SSM_GROUP)
    to_b = lambda t: jnp.einsum("wgpc,gh->wgchp", t.reshape(shp), eye)
    bw = jnp.stack([to_b(bbr), to_b(bbi)], axis=3).reshape(SSM_WIN, LANES, 2 * SSM_WIN_STATE)
    shc = (SSM_WIN, gl, SSM_GROUP, SSM_STATE)
    to_c = lambda t: jnp.einsum("wgcp,gh->wgphc", t.reshape(shc), eye)
    cw = jnp.stack([to_c(c_re), -to_c(c_im)], axis=1).reshape(SSM_WIN, 2 * SSM_WIN_STATE, LANES)
    lr = jnp.broadcast_to(lbr.reshape(1, -1), (bsz, SSM_GROUPS * SSM_STATE))
    li = jnp.broadcast_to(lbi.reshape(1, -1), (bsz, SSM_GROUPS * SSM_STATE))
    return bw.astype(BF16), lr, li, cw.astype(BF16)


def _rope_tables(positions):
    half = ROT_DIM // 2
    inv_freq = ROPE_THETA ** (-jnp.arange(0, ROT_DIM, 2, dtype=F32) / ROT_DIM)
    ang = positions.astype(F32)[..., None] * inv_freq
    lane = np.arange(LANES) % DA_QK_DIM
    hit = (lane[None, :] % half) == np.arange(half)[:, None]
    spread = lambda t, sel: jnp.dot(t, jnp.asarray(sel, F32), precision=lax.Precision.HIGHEST)
    cosf = spread(jnp.cos(ang), hit & (lane < ROT_DIM)) + jnp.asarray(lane >= ROT_DIM, F32)
    sina = spread(jnp.sin(ang), hit & (lane >= half) & (lane < ROT_DIM))
    sinb = spread(-jnp.sin(ang), hit & (lane < half))
    return cosf, sina, sinb


def kernel(x, mem, positions, norm_mix, w_in, ssm_lam_re, ssm_lam_im, ssm_log_dt, ssm_b_re, ssm_b_im, ssm_c_re, ssm_c_im, ssm_d, ssm_w_glu, diff_lambda, diff_subln, w_branch, w_out, norm_cross, norm_mem, w_xq, w_xkv, w_xo, norm_mlp, w_up, w_down, norm_final):
    bsz, seq, _ = x.shape
    depth = w_in.shape[0]
    ts = min(64, seq)
    sb_tq = min(256, seq)
    da_tq = min(512, seq)
    tm = min(512, seq)

    rope = _rope_tables(positions)
    kv_all = _memkv(mem, norm_mem, w_xkv.astype(BF16))
    wa = w_in[:, :, :N_PROJ_A].astype(BF16)
    wg = w_in[:, :, N_PROJ_A:].astype(BF16)
    wglu, wbr, wout = ssm_w_glu.astype(BF16), w_branch.astype(BF16), w_out.astype(BF16)
    wxq, wxo, wup, wdown = w_xq.astype(BF16), w_xo.astype(BF16), w_up.astype(BF16), w_down.astype(BF16)

    for l in range(depth):
        ssm = _ssm_tables(ssm_lam_re[l], ssm_lam_im[l], ssm_log_dt[l], ssm_b_re[l], ssm_b_im[l],
                          ssm_c_re[l], ssm_c_im[l], bsz)
        qkv, yssm = _inproj_ssm(x, norm_mix[l], wa, rope, ssm, ssm_d[l], wglu, l, ts=ts)
        ysb = _sb_attention(qkv, tq=sb_tq, npair=4)
        lam_init = 0.8 - 0.6 * math.exp(-0.3 * l)
        yda = _da_attention(qkv, diff_lambda[l], diff_subln[l], lam_init, tq=da_tq, nhd=2)
        x = _merge_cross_mlp(x, yssm, ysb, yda, norm_mix[l], wg, wbr, wout, norm_cross[l], wxq, kv_all, wxo,
                             norm_mlp[l], wup, wdown, norm_final, l, final=(l == depth - 1), tm=tm)
    return x
```

```python
import functools
import math

import jax
import jax.numpy as jnp
import numpy as np
from jax import lax
from jax.experimental import pallas as pl
from jax.experimental.pallas import tpu as pltpu

F32 = jnp.float32
BF16 = jnp.bfloat16

D_MODEL = 1024
MEM_LEN = 256
RMS_EPS = 1e-6
SSM_WIDTH = 512
SSM_GROUP = 16
SSM_GROUPS = 32
SSM_STATE = 64
SB_HEADS = 8
SB_HEAD_DIM = 64
SB_WIDTH = 512
DA_HEADS = 4
DA_QK_DIM = 64
DA_V_DIM = 128
DA_QK_WIDTH = 512
DA_WIDTH = 512
ROT_DIM = 16
ROPE_THETA = 500000.0
N_BRANCH = 3
BRANCH_WIDTH = 512
XA_HEADS = 4
XA_HEAD_DIM = 128
XA_WIDTH = 512
MLP_HIDDEN = 4096

LANES = 128
SUBLANES = 8
VMEM_LIMIT_BYTES = 58 * 1024 * 1024

N_PROJ_A = SSM_WIDTH + 3 * SB_WIDTH + 2 * DA_QK_WIDTH + DA_WIDTH
N_QKV = N_PROJ_A - SSM_WIDTH
SBQ_BLK, SBK_BLK, SBV_BLK = 0, 4, 8
DAQ_BLK, DAK_BLK, DAV_BLK = 12, 16, 20

SSM_WIN = SSM_WIDTH // LANES
SSM_WIN_STATE = (SSM_GROUPS // SSM_WIN) * SSM_STATE

LOG2_E = 1.4426950408889634
SB_LOG2_REACH = 150.0
MASK_VALUE = -1e30
DA_ONES_ROWS = 16


def _rms(x, gain):
    ms = jnp.mean(x * x, axis=-1, keepdims=True)
    return x * lax.rsqrt(ms + RMS_EPS) * gain


def _dot(a, b):
    return jnp.dot(a, b, preferred_element_type=F32)


def _dot_nt(a, b):
    return lax.dot_general(a, b, (((1,), (1,)), ((), ())), preferred_element_type=F32)


def _params(*sem):
    return pltpu.CompilerParams(dimension_semantics=sem, vmem_limit_bytes=VMEM_LIMIT_BYTES)


def _const_spec(shape):
    nd = len(shape)
    return pl.BlockSpec(shape, lambda *_: (0,) * nd, pipeline_mode=pl.Buffered(1))


def _layer_spec(shape, layer):
    nd = len(shape)
    return pl.BlockSpec((None,) + tuple(shape), lambda *_: (layer,) + (0,) * nd, pipeline_mode=pl.Buffered(1))


def _memkv_kernel(mem_ref, g_ref, w_ref, o_ref):
    h = _rms(mem_ref[0], g_ref[0]).astype(BF16)
    o_ref[0, 0] = _dot(h, w_ref[0]).astype(BF16)


def _memkv(mem, norm_mem, w_xkv_bf):
    depth = w_xkv_bf.shape[0]
    bsz = mem.shape[0]
    return pl.pallas_call(
        _memkv_kernel,
        grid=(depth, bsz),
        in_specs=[
            pl.BlockSpec((1, MEM_LEN, D_MODEL), lambda l, b: (b, 0, 0)),
            pl.BlockSpec((1, 1, D_MODEL), lambda l, b: (l, 0, 0)),
            pl.BlockSpec((1, D_MODEL, 2 * XA_WIDTH), lambda l, b: (l, 0, 0)),
        ],
        out_specs=pl.BlockSpec((1, 1, MEM_LEN, 2 * XA_WIDTH), lambda l, b: (l, b, 0, 0)),
        out_shape=jax.ShapeDtypeStruct((depth, bsz, MEM_LEN, 2 * XA_WIDTH), BF16),
        compiler_params=_params("arbitrary", "arbitrary"),
        name="memkv",
    )(mem, norm_mem.reshape(depth, 1, D_MODEL), w_xkv_bf)


def _inproj_ssm_kernel(x_ref, g_ref, wa_ref, rope_ref, bw_ref, lr_ref, li_ref,
                       cw_ref, d_ref, wglu_ref, qkv_ref, yssm_ref, u_tb, bux, st, y_tb, *, ts, bsz):
    @pl.when(pl.program_id(0) == 0)
    def _():
        st[...] = jnp.zeros_like(st)

    m = ts * bsz
    h = _rms(x_ref[...].reshape(m, D_MODEL), g_ref[...]).astype(BF16)

    def proj(c0, width=SSM_WIDTH):
        return _dot(h, wa_ref[:, c0:c0 + width])

    def put(dst, val):
        qkv_ref[:, :, dst:dst + val.shape[1]] = val.astype(BF16).reshape(bsz, ts, val.shape[1])

    u = proj(0)
    for b in range(bsz):
        for w in range(SSM_WIN):
            u_tb[w, pl.ds(b, ts, stride=bsz), :] = u[ts * b:ts * (b + 1), LANES * w:LANES * (w + 1)]

    wst = 2 * SSM_WIN_STATE
    for w in range(SSM_WIN):
        bux[:, wst * w:wst * (w + 1)] = _dot(u_tb[w].astype(BF16), bw_ref[w])

    nhalf = 2 * SSM_WIN
    hs = SSM_WIN_STATE
    state = [st[:, hs * k:hs * (k + 1)] for k in range(nhalf)]
    for t in range(ts):
        rows = slice(bsz * t, bsz * (t + 1))
        for w in range(SSM_WIN):
            xr, xi = state[2 * w], state[2 * w + 1]
            lr = lr_ref[:, hs * w:hs * (w + 1)]
            li = li_ref[:, hs * w:hs * (w + 1)]
            cr = slice(hs * 2 * w, hs * (2 * w + 1))
            ci = slice(hs * (2 * w + 1), hs * (2 * w + 2))
            state[2 * w] = lr * xr - li * xi + bux[rows, cr]
            state[2 * w + 1] = lr * xi + li * xr + bux[rows, ci]
            bux[rows, cr] = state[2 * w]
            bux[rows, ci] = state[2 * w + 1]
    for k in range(nhalf):
        st[:, hs * k:hs * (k + 1)] = state[k]

    o = SSM_WIDTH
    put(0, proj(o) * (SB_HEAD_DIM ** -0.5 * LOG2_E))
    put(SB_WIDTH, proj(o + SB_WIDTH))
    put(2 * SB_WIDTH, proj(o + 2 * SB_WIDTH))
    rope = rope_ref[...].reshape(m, 3 * LANES)
    cosf, sina, sinb = rope[:, :LANES], rope[:, LANES:2 * LANES], rope[:, 2 * LANES:]
    src = o + 3 * SB_WIDTH
    dst = 3 * SB_WIDTH
    for scale in (DA_QK_DIM ** -0.5 * LOG2_E, 1.0):
        t4 = proj(src)
        for hh in range(DA_HEADS):
            t = t4[:, LANES * hh:LANES * (hh + 1)]
            r = (t * cosf + pltpu.roll(t, ROT_DIM // 2, 1) * sina
                 + pltpu.roll(t, LANES - ROT_DIM // 2, 1) * sinb)
            put(dst + LANES * hh, r * scale)
        src += DA_QK_WIDTH
        dst += DA_QK_WIDTH
    put(dst, proj(src))

    ys = []
    for w in range(SSM_WIN):
        xw = bux[:, wst * w:wst * (w + 1)].astype(BF16)
        yw = _dot(xw, cw_ref[w])
        yw = yw + d_ref[:, LANES * w:LANES * (w + 1)] * u_tb[w]
        ys.append(jax.nn.gelu(yw).astype(BF16))
    glu = _dot(jnp.concatenate(ys, axis=1), wglu_ref[...])
    for w in range(SSM_WIN):
        sl = slice(LANES * w, LANES * (w + 1))
        y_tb[w] = glu[:, sl] * jax.nn.sigmoid(glu[:, SSM_WIDTH + LANES * w:SSM_WIDTH + LANES * (w + 1)])
    for b in range(bsz):
        for w in range(SSM_WIN):
            yssm_ref[b, :, LANES * w:LANES * (w + 1)] = y_tb[w, pl.ds(b, ts, stride=bsz), :].astype(BF16)


def _inproj_ssm(x, gain, wa, rope, ssm, d_skip, w_glu, layer, *, ts):
    bsz, seq, _ = x.shape
    assert bsz == SUBLANES and seq % ts == 0
    bw, lr, li, cw = ssm
    m = ts * bsz
    tile = lambda width: pl.BlockSpec((bsz, ts, width), lambda i: (0, i, 0))
    return pl.pallas_call(
        functools.partial(_inproj_ssm_kernel, ts=ts, bsz=bsz),
        grid=(seq // ts,),
        in_specs=[
            tile(D_MODEL),
            _const_spec((1, D_MODEL)),
            _layer_spec((D_MODEL, N_PROJ_A), layer),
            tile(3 * LANES),
            _const_spec(bw.shape), _const_spec(lr.shape), _const_spec(li.shape), _const_spec(cw.shape),
            _const_spec((1, SSM_WIDTH)),
            _layer_spec((SSM_WIDTH, 2 * SSM_WIDTH), layer),
        ],
        out_specs=[tile(N_QKV), tile(SSM_WIDTH)],
        out_shape=[jax.ShapeDtypeStruct((bsz, seq, N_QKV), BF16),
                   jax.ShapeDtypeStruct((bsz, seq, SSM_WIDTH), BF16)],
        scratch_shapes=[
            pltpu.VMEM((SSM_WIN, m, LANES), F32),
            pltpu.VMEM((m, 2 * SSM_WIN * SSM_WIN_STATE), F32),
            pltpu.VMEM((bsz, 2 * SSM_WIN * SSM_WIN_STATE), F32),
            pltpu.VMEM((SSM_WIN, m, LANES), F32),
        ],
        compiler_params=_params("arbitrary"),
        name="inproj_ssm",
    )(x, gain.reshape(1, D_MODEL), wa, rope, bw, lr, li, cw,
      d_skip.reshape(1, SSM_WIDTH), w_glu)


def _sb_kernel(q_ref, k_ref, v_ref, o_ref, vt, *, tq, npair):
    i = pl.program_id(2)
    tk = tq
    nhead = 2 * npair
    blk = lambda p: slice(LANES * p, LANES * (p + 1))

    @pl.when(i == 0)
    def _():
        for p in range(npair):
            for c in range(vt.shape[1]):
                vt[p, c] = v_ref[0, tk * c:tk * (c + 1), blk(p)].astype(F32).T.astype(BF16)

    lane = lax.broadcasted_iota(jnp.int32, (tq, LANES), 1)
    qs = []
    for p in range(npair):
        q2 = q_ref[0, :, blk(p)]
        zero = jnp.zeros_like(q2)
        qs += [jnp.where(lane < SB_HEAD_DIM, q2, zero), jnp.where(lane >= SB_HEAD_DIM, q2, zero)]
    key = lax.broadcasted_iota(jnp.int32, (tk, tq), 0)
    qry = lax.broadcasted_iota(jnp.int32, (tk, tq), 1)
    strict = key < qry
    tri = (lax.broadcasted_iota(jnp.int32, (tk, tk), 1)
           > lax.broadcasted_iota(jnp.int32, (tk, tk), 0)).astype(BF16)

    def blocks(js, masks, runs):
        rows = [pl.ds(pl.multiple_of(j * tk, tk), tk) for j in js]
        z = [[_dot_nt(k_ref[0, r, blk(a // 2)], qs[a]) for a in range(nhead)] for r in rows]
        sp, hi = [], []
        for n, masked in enumerate(masks):
            if masked:
                z[n] = [jnp.where(strict, t, MASK_VALUE) for t in z[n]]
            row = []
            for a in range(nhead):
                t = z[n][a]
                row.append(jnp.maximum(t, 0.0) + jnp.log2(1.0 + jnp.exp2(-jnp.abs(t))))
            sp.append(row)
            hi.append([t.astype(BF16) for t in row])
        cum = [[_dot(tri, hi[n][a]) for a in range(nhead)] for n in range(len(js))]
        w = []
        runs = list(runs)
        for n in range(len(js)):
            row = []
            for a in range(nhead):
                row.append(jnp.exp2(z[n][a] - sp[n][a] - (cum[n][a] + runs[a])).astype(BF16))
                runs[a] = runs[a] + jnp.sum(sp[n][a], axis=0, keepdims=True)
            w.append(row)
        pvs = [[_dot(vt[a // 2, j], w[n][a]) for a in range(nhead)] for n, j in enumerate(js)]
        return pvs, runs

    zrun = jnp.zeros((1, tq), F32)
    (accs, prev), runs = blocks((i, jnp.maximum(i - 1, 0)), (True, False), [zrun] * nhead)
    live = i > 0
    accs = [a + jnp.where(live, p, 0.0) for a, p in zip(accs, prev)]

    def cond(c):
        j, runs, _ = c
        return jnp.logical_and(j >= 0, jnp.min(functools.reduce(jnp.minimum, runs)) < SB_LOG2_REACH)

    def body(c):
        j, runs, accs = c
        (pvs,), runs = blocks((j,), (False,), runs)
        return j - 1, tuple(runs), tuple(a + p for a, p in zip(accs, pvs))

    _, _, accs = lax.while_loop(cond, body, (i - 2, tuple(runs), tuple(accs)))
    dim = lax.broadcasted_iota(jnp.int32, (LANES, tq), 0)
    for p in range(npair):
        o_ref[0, :, blk(p)] = jnp.where(dim < SB_HEAD_DIM, accs[2 * p], accs[2 * p + 1]).T.astype(BF16)


def _sb_attention(qkv, *, tq, npair):
    bsz, seq, _ = qkv.shape
    width = LANES * npair
    ngroup = SB_WIDTH // width
    col = lambda first: (lambda b, g, i: (b, 0, first * LANES // width + g))
    return pl.pallas_call(
        functools.partial(_sb_kernel, tq=tq, npair=npair),
        grid=(bsz, ngroup, seq // tq),
        in_specs=[
            pl.BlockSpec((1, tq, width), lambda b, g, i: (b, i, SBQ_BLK * LANES // width + g)),
            pl.BlockSpec((1, seq, width), col(SBK_BLK)),
            pl.BlockSpec((1, seq, width), col(SBV_BLK)),
        ],
        out_specs=pl.BlockSpec((1, tq, width), lambda b, g, i: (b, i, g)),
        out_shape=jax.ShapeDtypeStruct((bsz, seq, SB_WIDTH), BF16),
        scratch_shapes=[pltpu.VMEM((npair, seq // tq, LANES, tq), BF16)],
        compiler_params=_params("arbitrary", "arbitrary", "arbitrary"),
        name="sb_attn",
    )(qkv, qkv, qkv)


def _da_kernel(q_ref, k_ref, v_ref, lam_ref, g_ref, o_ref, vt, sa, sb, acc, *, tq, th, nhd, lam_init):
    i = pl.program_id(2)
    nchain = 2 * nhd
    blk = lambda hh: slice(LANES * hh, LANES * (hh + 1))

    @pl.when(i == 0)
    def _():
        for hh in range(nhd):
            for c in range(vt.shape[1]):
                vt[hh, c, :DA_V_DIM, :] = v_ref[0, th * c:th * (c + 1), blk(hh)].astype(F32).T.astype(BF16)
                vt[hh, c, DA_V_DIM:, :] = jnp.ones((DA_ONES_ROWS, th), BF16)

    lane = lax.broadcasted_iota(jnp.int32, (tq, LANES), 1)

    def queries(tile):
        rows = pl.ds(pl.multiple_of(tile * tq, tq), tq)
        out = []
        for hh in range(nhd):
            q = q_ref[0, rows, blk(hh)]
            zero = jnp.zeros_like(q)
            out += [jnp.where(lane < DA_QK_DIM, q, zero), jnp.where(lane >= DA_QK_DIM, q, zero)]
        return out

    qs = queries(i)

    def scores_to(buf, kblk, qs=qs):
        rows = pl.ds(pl.multiple_of(kblk * th, th), th)
        for x in range(nchain):
            buf[x] = _dot_nt(k_ref[0, rows, blk(x // 2)], qs[x])

    @pl.when(i == 0)
    def _():
        scores_to(sa, 0)

    def update(buf, kblk, stats, masked):
        if masked:
            key = lax.broadcasted_iota(jnp.int32, (th, tq), 0) + kblk * th
            qry = lax.broadcasted_iota(jnp.int32, (th, tq), 1) + i * tq
            causal = key <= qry
        new, ps = [], []
        for x in range(nchain):
            s = buf[x]
            if masked:
                s = jnp.where(causal, s, MASK_VALUE)
            mn = jnp.maximum(stats[x], jnp.max(s, axis=0, keepdims=True))
            ps.append(jnp.exp2(s - mn).astype(BF16))
            new.append(mn)
        for x in range(nchain):
            acc[x] = jnp.exp2(stats[x] - new[x]) * acc[x] + _dot(vt[x // 2, kblk], ps[x])
        return tuple(new)

    def pair(j, stats, masked):
        scores_to(sb, 2 * j + 1)
        stats = update(sa, 2 * j, stats, masked)
        if masked:
            scores_to(sa, 0, queries(jnp.minimum(i + 1, q_ref.shape[1] // tq - 1)))
        else:
            scores_to(sa, 2 * j + 2)
        return update(sb, 2 * j + 1, stats, masked)

    acc[...] = jnp.zeros_like(acc)
    init = tuple(jnp.full((1, tq), MASK_VALUE, F32) for _ in range(nchain))
    stats = lax.fori_loop(0, i, lambda j, st: pair(j, st, False), init)
    pair(i, stats, True)

    lv = lam_ref[...]
    lam = (jnp.exp(jnp.sum(lv[0:1] * lv[1:2], axis=1, keepdims=True))
           - jnp.exp(jnp.sum(lv[2:3] * lv[3:4], axis=1, keepdims=True)) + lam_init)
    for hh in range(nhd):
        l1 = acc[2 * hh, DA_V_DIM:DA_V_DIM + 1, :]
        l2 = acc[2 * hh + 1, DA_V_DIM:DA_V_DIM + 1, :]
        ot = acc[2 * hh, :DA_V_DIM, :] * (1.0 / l1) - acc[2 * hh + 1, :DA_V_DIM, :] * (lam / l2)
        ms = jnp.mean(ot * ot, axis=0, keepdims=True)
        ot = ot * (lax.rsqrt(ms + RMS_EPS) * (1.0 - lam_init)) * g_ref[...]
        o_ref[0, :, blk(hh)] = ot.T.astype(BF16)


def _da_attention(qkv, diff_lambda, diff_subln, lam_init, *, tq, nhd):
    bsz, seq, _ = qkv.shape
    th = tq // 2
    width = LANES * nhd
    assert seq % tq == 0 and DA_HEADS % nhd == 0
    col = lambda first: (lambda b, g, i: (b, 0, first * LANES // width + g))
    return pl.pallas_call(
        functools.partial(_da_kernel, tq=tq, th=th, nhd=nhd, lam_init=lam_init),
        grid=(bsz, DA_HEADS // nhd, seq // tq),
        in_specs=[
            pl.BlockSpec((1, seq, width), col(DAQ_BLK)),
            pl.BlockSpec((1, seq, width), col(DAK_BLK)),
            pl.BlockSpec((1, seq, width), col(DAV_BLK)),
            pl.BlockSpec((4, DA_QK_DIM), lambda b, g, i: (0, 0)),
            pl.BlockSpec((DA_V_DIM, 1), lambda b, g, i: (0, 0)),
        ],
        out_specs=pl.BlockSpec((1, tq, width), lambda b, g, i: (b, i, g)),
        out_shape=jax.ShapeDtypeStruct((bsz, seq, DA_WIDTH), BF16),
        scratch_shapes=[pltpu.VMEM((nhd, seq // th, DA_V_DIM + DA_ONES_ROWS, th), BF16),
                        pltpu.VMEM((2 * nhd, th, tq), F32), pltpu.VMEM((2 * nhd, th, tq), F32),
                        pltpu.VMEM((2 * nhd, DA_V_DIM + DA_ONES_ROWS, tq), F32)],
        compiler_params=_params("arbitrary", "arbitrary", "arbitrary"),
        name="da_attn",
    )(qkv, qkv, qkv, diff_lambda, diff_subln.reshape(DA_V_DIM, 1))


def _merge_cross_mlp_kernel(x_ref, yssm_ref, ysb_ref, yda_ref, gmix_ref, wg_ref, wbr_ref, wout_ref,
                            gcross_ref, wxq_ref, kv_ref, wxo_ref, gmlp_ref, wup_ref, wdown_ref, gfin_ref,
                            o_ref, *, final, chunk):
    x = x_ref[0]
    h = _rms(x, gmix_ref[...]).astype(BF16)
    merged = None
    for n, y_ref in enumerate((yssm_ref, ysb_ref, yda_ref)):
        gate = jax.nn.sigmoid(_dot(h, wg_ref[:, D_MODEL * n:D_MODEL * (n + 1)]))
        term = gate * _dot(y_ref[0], wbr_ref[n])
        merged = term if merged is None else merged + term
    x1 = x + _dot(merged.astype(BF16), wout_ref[...])

    hx = _rms(x1, gcross_ref[...]).astype(BF16)
    xq = _dot(hx, wxq_ref[...]).astype(BF16)
    kv = kv_ref[0]
    heads = []
    for hh in range(XA_HEADS):
        sl = slice(XA_HEAD_DIM * hh, XA_HEAD_DIM * (hh + 1))
        s = _dot_nt(xq[:, sl], kv[:, sl]) * XA_HEAD_DIM ** -0.5
        p = jnp.exp(s - jnp.max(s, axis=1, keepdims=True))
        vh = kv[:, XA_WIDTH + XA_HEAD_DIM * hh:XA_WIDTH + XA_HEAD_DIM * (hh + 1)]
        heads.append((_dot(p.astype(BF16), vh) / jnp.sum(p, axis=1, keepdims=True)).astype(BF16))
    x2 = x1 + _dot(jnp.concatenate(heads, axis=1), wxo_ref[...])

    hm = _rms(x2, gmlp_ref[...]).astype(BF16)
    acc = x2
    for c in range(MLP_HIDDEN // chunk):
        up = _dot(hm, wup_ref[:, chunk * c:chunk * (c + 1)])
        act = jnp.square(jnp.maximum(up, 0.0)).astype(BF16)
        acc = acc + _dot(act, wdown_ref[chunk * c:chunk * (c + 1), :])
    if final:
        acc = _rms(acc, gfin_ref[...])
    o_ref[0] = acc


def _merge_cross_mlp(x, yssm, ysb, yda, gmix, wg, wbr, wout, gcross, wxq, kv, wxo, gmlp, wup, wdown, gfin,
                     layer, *, final, tm):
    bsz, seq, _ = x.shape
    tile = lambda width: pl.BlockSpec((1, tm, width), lambda b, i: (b, i, 0))
    gain = lambda g: g.reshape(1, D_MODEL)
    return pl.pallas_call(
        functools.partial(_merge_cross_mlp_kernel, final=final, chunk=1024),
        grid=(bsz, seq // tm),
        in_specs=[
            tile(D_MODEL), tile(BRANCH_WIDTH), tile(BRANCH_WIDTH), tile(BRANCH_WIDTH),
            _const_spec((1, D_MODEL)),
            _layer_spec((D_MODEL, N_BRANCH * D_MODEL), layer),
            _layer_spec((N_BRANCH, BRANCH_WIDTH, D_MODEL), layer),
            _layer_spec((D_MODEL, D_MODEL), layer),
            _const_spec((1, D_MODEL)),
            _layer_spec((D_MODEL, XA_WIDTH), layer),
            pl.BlockSpec((None, 1, MEM_LEN, 2 * XA_WIDTH), lambda b, i: (layer, b, 0, 0)),
            _layer_spec((XA_WIDTH, D_MODEL), layer),
            _const_spec((1, D_MODEL)),
            _layer_spec((D_MODEL, MLP_HIDDEN), layer),
            _layer_spec((MLP_HIDDEN, D_MODEL), layer),
            _const_spec((1, D_MODEL)),
        ],
        out_specs=tile(D_MODEL),
        out_shape=jax.ShapeDtypeStruct((bsz, seq, D_MODEL), F32),
        compiler_params=_params("parallel", "arbitrary"),
        name="merge_cross_mlp",
    )(x, yssm, ysb, yda, gain(gmix), wg, wbr, wout, gain(gcross), wxq, kv, wxo, gain(gmlp), wup, wdown, gain(gfin))


def _ssm_tables(lam_re, lam_im, log_dt, b_re, b_im, c_re, c_im, bsz):
    dt = jnp.exp(log_dt)[:, None]
    mag = jnp.exp(lam_re * dt)
    lbr = mag * jnp.cos(lam_im * dt)
    lbi = mag * jnp.sin(lam_im * dt)
    den = lam_re * lam_re + lam_im * lam_im
    fr = ((lbr - 1.0) * lam_re + lbi * lam_im) / den
    fi = (lbi * lam_re - (lbr - 1.0) * lam_im) / den
    bbr = fr[..., None] * b_re - fi[..., None] * b_im
    bbi = fr[..., None] * b_im + fi[..., None] * b_re
    gl = SSM_GROUPS // SSM_WIN
    eye = jnp.eye(gl, dtype=F32)
    shp = (SSM_WIN, gl, SSM_STATE, SSM_GROUP)
    to_b = lambda t: jnp.einsum("wgpc,gh->wgchp", t.reshape(shp), eye)
    bw = jnp.stack([to_b(bbr), to_b(bbi)], axis=3).reshape(SSM_WIN, LANES, 2 * SSM_WIN_STATE)
    shc = (SSM_WIN, gl, SSM_GROUP, SSM_STATE)
    to_c = lambda t: jnp.einsum("wgcp,gh->wgphc", t.reshape(shc), eye)
    cw = jnp.stack([to_c(c_re), -to_c(c_im)], axis=1).reshape(SSM_WIN, 2 * SSM_WIN_STATE, LANES)
    lr = jnp.broadcast_to(lbr.reshape(1, -1), (bsz, SSM_GROUPS * SSM_STATE))
    li = jnp.broadcast_to(lbi.reshape(1, -1), (bsz, SSM_GROUPS * SSM_STATE))
    return bw.astype(BF16), lr, li, cw.astype(BF16)


def _rope_tables(positions):
    half = ROT_DIM // 2
    inv_freq = ROPE_THETA ** (-jnp.arange(0, ROT_DIM, 2, dtype=F32) / ROT_DIM)
    ang = positions.astype(F32)[..., None] * inv_freq
    lane = np.arange(LANES) % DA_QK_DIM
    hit = (lane[None, :] % half) == np.arange(half)[:, None]
    none = np.zeros_like(hit)
    sel = np.block([[hit & (lane < ROT_DIM), none, none],
                    [none, hit & (lane >= half) & (lane < ROT_DIM), hit & (lane < half)]])
    sign = np.concatenate([np.ones(2 * LANES), -np.ones(LANES)])
    one = np.concatenate([lane >= ROT_DIM, np.zeros(2 * LANES, bool)])
    cs = jnp.concatenate([jnp.cos(ang), jnp.sin(ang)], axis=-1)
    return (jnp.dot(cs, jnp.asarray(sel * sign, F32), precision=lax.Precision.HIGHEST)
            + jnp.asarray(one, F32))


def kernel(x, mem, positions, norm_mix, w_in, ssm_lam_re, ssm_lam_im, ssm_log_dt, ssm_b_re, ssm_b_im, ssm_c_re, ssm_c_im, ssm_d, ssm_w_glu, diff_lambda, diff_subln, w_branch, w_out, norm_cross, norm_mem, w_xq, w_xkv, w_xo, norm_mlp, w_up, w_down, norm_final):
    bsz, seq, _ = x.shape
    depth = w_in.shape[0]
    ts = min(64, seq)
    sb_tq = min(256, seq)
    da_tq = min(512, seq)
    tm = min(512, seq)

    rope = _rope_tables(positions)
    kv_all = _memkv(mem, norm_mem, w_xkv.astype(BF16))
    wa = w_in[:, :, :N_PROJ_A].astype(BF16)
    wg = w_in[:, :, N_PROJ_A:].astype(BF16)
    wglu, wbr, wout = ssm_w_glu.astype(BF16), w_branch.astype(BF16), w_out.astype(BF16)
    wxq, wxo, wup, wdown = w_xq.astype(BF16), w_xo.astype(BF16), w_up.astype(BF16), w_down.astype(BF16)

    for l in range(depth):
        ssm = _ssm_tables(ssm_lam_re[l], ssm_lam_im[l], ssm_log_dt[l], ssm_b_re[l], ssm_b_im[l],
                          ssm_c_re[l], ssm_c_im[l], bsz)
        qkv, yssm = _inproj_ssm(x, norm_mix[l], wa, rope, ssm, ssm_d[l], wglu, l, ts=ts)
        ysb = _sb_attention(qkv, tq=sb_tq, npair=4)
        lam_init = 0.8 - 0.6 * math.exp(-0.3 * l)
        yda = _da_attention(qkv, diff_lambda[l], diff_subln[l], lam_init, tq=da_tq, nhd=2)
        x = _merge_cross_mlp(x, yssm, ysb, yda, norm_mix[l], wg, wbr, wout, norm_cross[l], wxq, kv_all, wxo,
                             norm_mlp[l], wup, wdown, norm_final, l, final=(l == depth - 1), tm=tm)
    return x
```

```python
import functools
import math

import jax
import jax.numpy as jnp
import numpy as np
from jax import lax
from jax.experimental import pallas as pl
from jax.experimental.pallas import tpu as pltpu

F32 = jnp.float32
BF16 = jnp.bfloat16

D_MODEL = 1024
MEM_LEN = 256
RMS_EPS = 1e-6
SSM_WIDTH = 512
SSM_GROUP = 16
SSM_GROUPS = 32
SSM_STATE = 64
SB_HEADS = 8
SB_HEAD_DIM = 64
SB_WIDTH = 512
DA_HEADS = 4
DA_QK_DIM = 64
DA_V_DIM = 128
DA_QK_WIDTH = 512
DA_WIDTH = 512
ROT_DIM = 16
ROPE_THETA = 500000.0
N_BRANCH = 3
BRANCH_WIDTH = 512
XA_HEADS = 4
XA_HEAD_DIM = 128
XA_WIDTH = 512
MLP_HIDDEN = 4096

LANES = 128
SUBLANES = 8
VMEM_LIMIT_BYTES = 58 * 1024 * 1024

N_PROJ_A = SSM_WIDTH + 3 * SB_WIDTH + 2 * DA_QK_WIDTH + DA_WIDTH
N_QKV = N_PROJ_A - SSM_WIDTH
SBQ_BLK, SBK_BLK, SBV_BLK = 0, 4, 8
DAQ_BLK, DAK_BLK, DAV_BLK = 12, 16, 20

SSM_WIN = SSM_WIDTH // LANES
SSM_WIN_STATE = (SSM_GROUPS // SSM_WIN) * SSM_STATE

LOG2_E = 1.4426950408889634
SB_LOG2_REACH = 150.0
MASK_VALUE = -1e30
DA_ONES_ROWS = 16


def _rms(x, gain):
    ms = jnp.mean(x * x, axis=-1, keepdims=True)
    return x * lax.rsqrt(ms + RMS_EPS) * gain


def _dot(a, b):
    return jnp.dot(a, b, preferred_element_type=F32)


def _dot_nt(a, b):
    return lax.dot_general(a, b, (((1,), (1,)), ((), ())), preferred_element_type=F32)


def _params(*sem):
    return pltpu.CompilerParams(dimension_semantics=sem, vmem_limit_bytes=VMEM_LIMIT_BYTES)


def _const_spec(shape):
    nd = len(shape)
    return pl.BlockSpec(shape, lambda *_: (0,) * nd, pipeline_mode=pl.Buffered(1))


def _layer_spec(shape, layer):
    nd = len(shape)
    return pl.BlockSpec((None,) + tuple(shape), lambda *_: (layer,) + (0,) * nd, pipeline_mode=pl.Buffered(1))


def _memkv_kernel(mem_ref, g_ref, w_ref, o_ref):
    h = _rms(mem_ref[0], g_ref[0]).astype(BF16)
    o_ref[0, 0] = _dot(h, w_ref[0]).astype(BF16)


def _memkv(mem, norm_mem, w_xkv_bf):
    depth = w_xkv_bf.shape[0]
    bsz = mem.shape[0]
    return pl.pallas_call(
        _memkv_kernel,
        grid=(depth, bsz),
        in_specs=[
            pl.BlockSpec((1, MEM_LEN, D_MODEL), lambda l, b: (b, 0, 0)),
            pl.BlockSpec((1, 1, D_MODEL), lambda l, b: (l, 0, 0)),
            pl.BlockSpec((1, D_MODEL, 2 * XA_WIDTH), lambda l, b: (l, 0, 0)),
        ],
        out_specs=pl.BlockSpec((1, 1, MEM_LEN, 2 * XA_WIDTH), lambda l, b: (l, b, 0, 0)),
        out_shape=jax.ShapeDtypeStruct((depth, bsz, MEM_LEN, 2 * XA_WIDTH), BF16),
        compiler_params=_params("arbitrary", "arbitrary"),
        name="memkv",
    )(mem, norm_mem.reshape(depth, 1, D_MODEL), w_xkv_bf)


def _inproj_ssm_kernel(x_ref, g_ref, wa_ref, rope_ref, bw_ref, lr_ref, li_ref,
                       cw_ref, d_ref, wglu_ref, qkv_ref, yssm_ref, u_tb, bux, st, y_tb, *, ts, bsz):
    @pl.when(pl.program_id(0) == 0)
    def _():
        st[...] = jnp.zeros_like(st)

    m = ts * bsz
    h = _rms(x_ref[...].reshape(m, D_MODEL), g_ref[...]).astype(BF16)

    def proj(c0, width=SSM_WIDTH):
        return _dot(h, wa_ref[:, c0:c0 + width])

    def put(dst, val):
        qkv_ref[:, :, dst:dst + val.shape[1]] = val.astype(BF16).reshape(bsz, ts, val.shape[1])

    u = proj(0)
    for b in range(bsz):
        for w in range(SSM_WIN):
            u_tb[w, pl.ds(b, ts, stride=bsz), :] = u[ts * b:ts * (b + 1), LANES * w:LANES * (w + 1)]

    wst = 2 * SSM_WIN_STATE
    for w in range(SSM_WIN):
        bux[:, wst * w:wst * (w + 1)] = _dot(u_tb[w].astype(BF16), bw_ref[w])

    nhalf = 2 * SSM_WIN
    hs = SSM_WIN_STATE
    state = [st[:, hs * k:hs * (k + 1)] for k in range(nhalf)]
    for t in range(ts):
        rows = slice(bsz * t, bsz * (t + 1))
        for w in range(SSM_WIN):
            xr, xi = state[2 * w], state[2 * w + 1]
            lr = lr_ref[:, hs * w:hs * (w + 1)]
            li = li_ref[:, hs * w:hs * (w + 1)]
            cr = slice(hs * 2 * w, hs * (2 * w + 1))
            ci = slice(hs * (2 * w + 1), hs * (2 * w + 2))
            state[2 * w] = lr * xr - li * xi + bux[rows, cr]
            state[2 * w + 1] = lr * xi + li * xr + bux[rows, ci]
            bux[rows, cr] = state[2 * w]
            bux[rows, ci] = state[2 * w + 1]
    for k in range(nhalf):
        st[:, hs * k:hs * (k + 1)] = state[k]

    o = SSM_WIDTH
    put(0, proj(o) * (SB_HEAD_DIM ** -0.5 * LOG2_E))
    put(SB_WIDTH, proj(o + SB_WIDTH))
    put(2 * SB_WIDTH, proj(o + 2 * SB_WIDTH))
    rope = rope_ref[...].reshape(m, 3 * LANES)
    cosf, sina, sinb = rope[:, :LANES], rope[:, LANES:2 * LANES], rope[:, 2 * LANES:]
    src = o + 3 * SB_WIDTH
    dst = 3 * SB_WIDTH
    for scale in (DA_QK_DIM ** -0.5 * LOG2_E, 1.0):
        t4 = proj(src)
        for hh in range(DA_HEADS):
            t = t4[:, LANES * hh:LANES * (hh + 1)]
            r = (t * cosf + pltpu.roll(t, ROT_DIM // 2, 1) * sina
                 + pltpu.roll(t, LANES - ROT_DIM // 2, 1) * sinb)
            put(dst + LANES * hh, r * scale)
        src += DA_QK_WIDTH
        dst += DA_QK_WIDTH
    put(dst, proj(src))

    ys = []
    for w in range(SSM_WIN):
        xw = bux[:, wst * w:wst * (w + 1)].astype(BF16)
        yw = _dot(xw, cw_ref[w])
        yw = yw + d_ref[:, LANES * w:LANES * (w + 1)] * u_tb[w]
        ys.append(jax.nn.gelu(yw).astype(BF16))
    glu = _dot(jnp.concatenate(ys, axis=1), wglu_ref[...])
    for w in range(SSM_WIN):
        sl = slice(LANES * w, LANES * (w + 1))
        y_tb[w] = glu[:, sl] * jax.nn.sigmoid(glu[:, SSM_WIDTH + LANES * w:SSM_WIDTH + LANES * (w + 1)])
    for b in range(bsz):
        for w in range(SSM_WIN):
            yssm_ref[b, :, LANES * w:LANES * (w + 1)] = y_tb[w, pl.ds(b, ts, stride=bsz), :].astype(BF16)


def _inproj_ssm(x, gain, wa, rope, ssm, d_skip, w_glu, layer, *, ts):
    bsz, seq, _ = x.shape
    assert bsz == SUBLANES and seq % ts == 0
    bw, lr, li, cw = ssm
    m = ts * bsz
    tile = lambda width: pl.BlockSpec((bsz, ts, width), lambda i: (0, i, 0))
    return pl.pallas_call(
        functools.partial(_inproj_ssm_kernel, ts=ts, bsz=bsz),
        grid=(seq // ts,),
        in_specs=[
            tile(D_MODEL),
            _const_spec((1, D_MODEL)),
            _layer_spec((D_MODEL, N_PROJ_A), layer),
            tile(3 * LANES),
            _const_spec(bw.shape), _const_spec(lr.shape), _const_spec(li.shape), _const_spec(cw.shape),
            _const_spec((1, SSM_WIDTH)),
            _layer_spec((SSM_WIDTH, 2 * SSM_WIDTH), layer),
        ],
        out_specs=[tile(N_QKV), tile(SSM_WIDTH)],
        out_shape=[jax.ShapeDtypeStruct((bsz, seq, N_QKV), BF16),
                   jax.ShapeDtypeStruct((bsz, seq, SSM_WIDTH), BF16)],
        scratch_shapes=[
            pltpu.VMEM((SSM_WIN, m, LANES), F32),
            pltpu.VMEM((m, 2 * SSM_WIN * SSM_WIN_STATE), F32),
            pltpu.VMEM((bsz, 2 * SSM_WIN * SSM_WIN_STATE), F32),
            pltpu.VMEM((SSM_WIN, m, LANES), F32),
        ],
        compiler_params=_params("arbitrary"),
        name="inproj_ssm",
    )(x, gain.reshape(1, D_MODEL), wa, rope, bw, lr, li, cw,
      d_skip.reshape(1, SSM_WIDTH), w_glu)


def _sb_kernel(q_ref, k_ref, v_ref, o_ref, vt, *, tq, npair):
    i = pl.program_id(2)
    tk = tq
    nhead = 2 * npair
    blk = lambda p: slice(LANES * p, LANES * (p + 1))

    @pl.when(i == 0)
    def _():
        for p in range(npair):
            for c in range(vt.shape[1]):
                vt[p, c] = v_ref[0, tk * c:tk * (c + 1), blk(p)].astype(F32).T.astype(BF16)

    lane = lax.broadcasted_iota(jnp.int32, (tq, LANES), 1)
    qs = []
    for p in range(npair):
        q2 = q_ref[0, :, blk(p)]
        zero = jnp.zeros_like(q2)
        qs += [jnp.where(lane < SB_HEAD_DIM, q2, zero), jnp.where(lane >= SB_HEAD_DIM, q2, zero)]
    key = lax.broadcasted_iota(jnp.int32, (tk, tq), 0)
    qry = lax.broadcasted_iota(jnp.int32, (tk, tq), 1)
    strict = key < qry
    tri = (lax.broadcasted_iota(jnp.int32, (tk, tk), 1)
           > lax.broadcasted_iota(jnp.int32, (tk, tk), 0)).astype(BF16)

    def blocks(js, masks, runs):
        rows = [pl.ds(pl.multiple_of(j * tk, tk), tk) for j in js]
        z = [[_dot_nt(k_ref[0, r, blk(a // 2)], qs[a]) for a in range(nhead)] for r in rows]
        sp, hi = [], []
        for n, masked in enumerate(masks):
            if masked:
                z[n] = [jnp.where(strict, t, MASK_VALUE) for t in z[n]]
            row = []
            for a in range(nhead):
                t = z[n][a]
                row.append(jnp.maximum(t, 0.0) + jnp.log2(1.0 + jnp.exp2(-jnp.abs(t))))
            sp.append(row)
            hi.append([t.astype(BF16) for t in row])
        cum = [[_dot(tri, hi[n][a]) for a in range(nhead)] for n in range(len(js))]
        w = []
        runs = list(runs)
        for n in range(len(js)):
            row = []
            for a in range(nhead):
                row.append(jnp.exp2(z[n][a] - sp[n][a] - (cum[n][a] + runs[a])).astype(BF16))
                runs[a] = runs[a] + jnp.sum(sp[n][a], axis=0, keepdims=True)
            w.append(row)
        pvs = [[_dot(vt[a // 2, j], w[n][a]) for a in range(nhead)] for n, j in enumerate(js)]
        return pvs, runs

    zrun = jnp.zeros((1, tq), F32)
    (accs, prev), runs = blocks((i, jnp.maximum(i - 1, 0)), (True, False), [zrun] * nhead)
    live = i > 0
    accs = [a + jnp.where(live, p, 0.0) for a, p in zip(accs, prev)]

    def cond(c):
        j, runs, _ = c
        return jnp.logical_and(j >= 0, jnp.min(functools.reduce(jnp.minimum, runs)) < SB_LOG2_REACH)

    def body(c):
        j, runs, accs = c
        (pvs,), runs = blocks((j,), (False,), runs)
        return j - 1, tuple(runs), tuple(a + p for a, p in zip(accs, pvs))

    _, _, accs = lax.while_loop(cond, body, (i - 2, tuple(runs), tuple(accs)))
    dim = lax.broadcasted_iota(jnp.int32, (LANES, tq), 0)
    for p in range(npair):
        o_ref[0, :, blk(p)] = jnp.where(dim < SB_HEAD_DIM, accs[2 * p], accs[2 * p + 1]).T.astype(BF16)


def _sb_attention(qkv, *, tq, npair):
    bsz, seq, _ = qkv.shape
    width = LANES * npair
    ngroup = SB_WIDTH // width
    col = lambda first: (lambda b, g, i: (b, 0, first * LANES // width + g))
    return pl.pallas_call(
        functools.partial(_sb_kernel, tq=tq, npair=npair),
        grid=(bsz, ngroup, seq // tq),
        in_specs=[
            pl.BlockSpec((1, tq, width), lambda b, g, i: (b, i, SBQ_BLK * LANES // width + g)),
            pl.BlockSpec((1, seq, width), col(SBK_BLK)),
            pl.BlockSpec((1, seq, width), col(SBV_BLK)),
        ],
        out_specs=pl.BlockSpec((1, tq, width), lambda b, g, i: (b, i, g)),
        out_shape=jax.ShapeDtypeStruct((bsz, seq, SB_WIDTH), BF16),
        scratch_shapes=[pltpu.VMEM((npair, seq // tq, LANES, tq), BF16)],
        compiler_params=_params("arbitrary", "arbitrary", "arbitrary"),
        name="sb_attn",
    )(qkv, qkv, qkv)


def _da_kernel(q_ref, k_ref, v_ref, lam_ref, g_ref, o_ref, vt, sa, sb, mxa, mxb, acc, *, tq, th, nhd, lam_init):
    i = pl.program_id(2)
    nchain = 2 * nhd
    blk = lambda hh: slice(LANES * hh, LANES * (hh + 1))

    @pl.when(i == 0)
    def _():
        for hh in range(nhd):
            for c in range(vt.shape[1]):
                vt[hh, c, :DA_V_DIM, :] = v_ref[0, th * c:th * (c + 1), blk(hh)].astype(F32).T.astype(BF16)
                vt[hh, c, DA_V_DIM:, :] = jnp.ones((DA_ONES_ROWS, th), BF16)

    lane = lax.broadcasted_iota(jnp.int32, (tq, LANES), 1)

    def queries(tile):
        rows = pl.ds(pl.multiple_of(tile * tq, tq), tq)
        out = []
        for hh in range(nhd):
            q = q_ref[0, rows, blk(hh)]
            zero = jnp.zeros_like(q)
            out += [jnp.where(lane < DA_QK_DIM, q, zero), jnp.where(lane >= DA_QK_DIM, q, zero)]
        return out

    qs = queries(i)

    def scores_to(buf, mx, kblk, qs=qs):
        rows = pl.ds(pl.multiple_of(kblk * th, th), th)
        for x in range(nchain):
            s = _dot_nt(k_ref[0, rows, blk(x // 2)], qs[x])
            buf[x] = s
            mx[x] = jnp.max(s, axis=0, keepdims=True)

    @pl.when(i == 0)
    def _():
        scores_to(sa, mxa, 0)

    def update(buf, mx, kblk, stats, masked):
        if masked:
            key = lax.broadcasted_iota(jnp.int32, (th, tq), 0) + kblk * th
            qry = lax.broadcasted_iota(jnp.int32, (th, tq), 1) + i * tq
            causal = key <= qry
        new, ps = [], []
        for x in range(nchain):
            s = buf[x]
            if masked:
                s = jnp.where(causal, s, MASK_VALUE)
                top = jnp.max(s, axis=0, keepdims=True)
            else:
                top = mx[x]
            mn = jnp.maximum(stats[x], top)
            ps.append(jnp.exp2(s - mn).astype(BF16))
            new.append(mn)
        for x in range(nchain):
            acc[x] = jnp.exp2(stats[x] - new[x]) * acc[x] + _dot(vt[x // 2, kblk], ps[x])
        return tuple(new)

    def pair(j, stats, masked):
        scores_to(sb, mxb, 2 * j + 1)
        stats = update(sa, mxa, 2 * j, stats, masked)
        if masked:
            scores_to(sa, mxa, 0, queries(jnp.minimum(i + 1, q_ref.shape[1] // tq - 1)))
        else:
            scores_to(sa, mxa, 2 * j + 2)
        return update(sb, mxb, 2 * j + 1, stats, masked)

    acc[...] = jnp.zeros_like(acc)
    init = tuple(jnp.full((1, tq), MASK_VALUE, F32) for _ in range(nchain))
    stats = lax.fori_loop(0, i, lambda j, st: pair(j, st, False), init)
    pair(i, stats, True)

    lv = lam_ref[...]
    lam = (jnp.exp(jnp.sum(lv[0:1] * lv[1:2], axis=1, keepdims=True))
           - jnp.exp(jnp.sum(lv[2:3] * lv[3:4], axis=1, keepdims=True)) + lam_init)
    for hh in range(nhd):
        l1 = acc[2 * hh, DA_V_DIM:DA_V_DIM + 1, :]
        l2 = acc[2 * hh + 1, DA_V_DIM:DA_V_DIM + 1, :]
        ot = acc[2 * hh, :DA_V_DIM, :] * (1.0 / l1) - acc[2 * hh + 1, :DA_V_DIM, :] * (lam / l2)
        ms = jnp.mean(ot * ot, axis=0, keepdims=True)
        ot = ot * (lax.rsqrt(ms + RMS_EPS) * (1.0 - lam_init)) * g_ref[...]
        o_ref[0, :, blk(hh)] = ot.T.astype(BF16)


def _da_attention(qkv, diff_lambda, diff_subln, lam_init, *, tq, nhd):
    bsz, seq, _ = qkv.shape
    th = tq // 2
    width = LANES * nhd
    assert seq % tq == 0 and DA_HEADS % nhd == 0
    col = lambda first: (lambda b, g, i: (b, 0, first * LANES // width + g))
    return pl.pallas_call(
        functools.partial(_da_kernel, tq=tq, th=th, nhd=nhd, lam_init=lam_init),
        grid=(bsz, DA_HEADS // nhd, seq // tq),
        in_specs=[
            pl.BlockSpec((1, seq, width), col(DAQ_BLK)),
            pl.BlockSpec((1, seq, width), col(DAK_BLK)),
            pl.BlockSpec((1, seq, width), col(DAV_BLK)),
            pl.BlockSpec((4, DA_QK_DIM), lambda b, g, i: (0, 0)),
            pl.BlockSpec((DA_V_DIM, 1), lambda b, g, i: (0, 0)),
        ],
        out_specs=pl.BlockSpec((1, tq, width), lambda b, g, i: (b, i, g)),
        out_shape=jax.ShapeDtypeStruct((bsz, seq, DA_WIDTH), BF16),
        scratch_shapes=[pltpu.VMEM((nhd, seq // th, DA_V_DIM + DA_ONES_ROWS, th), BF16),
                        pltpu.VMEM((2 * nhd, th, tq), F32), pltpu.VMEM((2 * nhd, th, tq), F32),
                        pltpu.VMEM((2 * nhd, 1, tq), F32), pltpu.VMEM((2 * nhd, 1, tq), F32),
                        pltpu.VMEM((2 * nhd, DA_V_DIM + DA_ONES_ROWS, tq), F32)],
        compiler_params=_params("arbitrary", "arbitrary", "arbitrary"),
        name="da_attn",
    )(qkv, qkv, qkv, diff_lambda, diff_subln.reshape(DA_V_DIM, 1))


def _merge_cross_mlp_kernel(x_ref, yssm_ref, ysb_ref, yda_ref, gmix_ref, wg_ref, wbr_ref, wout_ref,
                            gcross_ref, wxq_ref, kv_ref, wxo_ref, gmlp_ref, wup_ref, wdown_ref, gfin_ref,
                            o_ref, *, final, chunk):
    x = x_ref[0]
    h = _rms(x, gmix_ref[...]).astype(BF16)
    merged = None
    for n, y_ref in enumerate((yssm_ref, ysb_ref, yda_ref)):
        gate = jax.nn.sigmoid(_dot(h, wg_ref[:, D_MODEL * n:D_MODEL * (n + 1)]))
        term = gate * _dot(y_ref[0], wbr_ref[n])
        merged = term if merged is None else merged + term
    x1 = x + _dot(merged.astype(BF16), wout_ref[...])

    hx = _rms(x1, gcross_ref[...]).astype(BF16)
    xq = _dot(hx, wxq_ref[...]).astype(BF16)
    kv = kv_ref[0]
    heads = []
    for hh in range(XA_HEADS):
        sl = slice(XA_HEAD_DIM * hh, XA_HEAD_DIM * (hh + 1))
        s = _dot_nt(xq[:, sl], kv[:, sl]) * XA_HEAD_DIM ** -0.5
        p = jnp.exp(s - jnp.max(s, axis=1, keepdims=True))
        vh = kv[:, XA_WIDTH + XA_HEAD_DIM * hh:XA_WIDTH + XA_HEAD_DIM * (hh + 1)]
        heads.append((_dot(p.astype(BF16), vh) / jnp.sum(p, axis=1, keepdims=True)).astype(BF16))
    x2 = x1 + _dot(jnp.concatenate(heads, axis=1), wxo_ref[...])

    hm = _rms(x2, gmlp_ref[...]).astype(BF16)
    acc = x2
    for c in range(MLP_HIDDEN // chunk):
        up = _dot(hm, wup_ref[:, chunk * c:chunk * (c + 1)])
        act = jnp.square(jnp.maximum(up, 0.0)).astype(BF16)
        acc = acc + _dot(act, wdown_ref[chunk * c:chunk * (c + 1), :])
    if final:
        acc = _rms(acc, gfin_ref[...])
    o_ref[0] = acc


def _merge_cross_mlp(x, yssm, ysb, yda, gmix, wg, wbr, wout, gcross, wxq, kv, wxo, gmlp, wup, wdown, gfin,
                     layer, *, final, tm):
    bsz, seq, _ = x.shape
    tile = lambda width: pl.BlockSpec((1, tm, width), lambda b, i: (b, i, 0))
    gain = lambda g: g.reshape(1, D_MODEL)
    return pl.pallas_call(
        functools.partial(_merge_cross_mlp_kernel, final=final, chunk=1024),
        grid=(bsz, seq // tm),
        in_specs=[
            tile(D_MODEL), tile(BRANCH_WIDTH), tile(BRANCH_WIDTH), tile(BRANCH_WIDTH),
            _const_spec((1, D_MODEL)),
            _layer_spec((D_MODEL, N_BRANCH * D_MODEL), layer),
            _layer_spec((N_BRANCH, BRANCH_WIDTH, D_MODEL), layer),
            _layer_spec((D_MODEL, D_MODEL), layer),
            _const_spec((1, D_MODEL)),
            _layer_spec((D_MODEL, XA_WIDTH), layer),
            pl.BlockSpec((None, 1, MEM_LEN, 2 * XA_WIDTH), lambda b, i: (layer, b, 0, 0)),
            _layer_spec((XA_WIDTH, D_MODEL), layer),
            _const_spec((1, D_MODEL)),
            _layer_spec((D_MODEL, MLP_HIDDEN), layer),
            _layer_spec((MLP_HIDDEN, D_MODEL), layer),
            _const_spec((1, D_MODEL)),
        ],
        out_specs=tile(D_MODEL),
        out_shape=jax.ShapeDtypeStruct((bsz, seq, D_MODEL), F32),
        compiler_params=_params("parallel", "arbitrary"),
        name="merge_cross_mlp",
    )(x, yssm, ysb, yda, gain(gmix), wg, wbr, wout, gain(gcross), wxq, kv, wxo, gain(gmlp), wup, wdown, gain(gfin))


def _ssm_tables(lam_re, lam_im, log_dt, b_re, b_im, c_re, c_im, bsz):
    dt = jnp.exp(log_dt)[:, None]
    mag = jnp.exp(lam_re * dt)
    lbr = mag * jnp.cos(lam_im * dt)
    lbi = mag * jnp.sin(lam_im * dt)
    den = lam_re * lam_re + lam_im * lam_im
    fr = ((lbr - 1.0) * lam_re + lbi * lam_im) / den
    fi = (lbi * lam_re - (lbr - 1.0) * lam_im) / den
    bbr = fr[..., None] * b_re - fi[..., None] * b_im
    bbi = fr[..., None] * b_im + fi[..., None] * b_re
    gl = SSM_GROUPS // SSM_WIN
    eye = jnp.eye(gl, dtype=F32)
    shp = (SSM_WIN, gl, SSM_STATE, SSM_GROUP)
    to_b = lambda t: jnp.einsum("wgpc,gh->wgchp", t.reshape(shp), eye)
    bw = jnp.stack([to_b(bbr), to_b(bbi)], axis=3).reshape(SSM_WIN, LANES, 2 * SSM_WIN_STATE)
    shc = (SSM_WIN, gl, SSM_GROUP, SSM_STATE)
    to_c = lambda t: jnp.einsum("wgcp,gh->wgphc", t.reshape(shc), eye)
    cw = jnp.stack([to_c(c_re), -to_c(c_im)], axis=1).reshape(SSM_WIN, 2 * SSM_WIN_STATE, LANES)
    lr = jnp.broadcast_to(lbr.reshape(1, -1), (bsz, SSM_GROUPS * SSM_STATE))
    li = jnp.broadcast_to(lbi.reshape(1, -1), (bsz, SSM_GROUPS * SSM_STATE))
    return bw.astype(BF16), lr, li, cw.astype(BF16)


def _rope_tables(positions):
    half = ROT_DIM // 2
    inv_freq = ROPE_THETA ** (-jnp.arange(0, ROT_DIM, 2, dtype=F32) / ROT_DIM)
    ang = positions.astype(F32)[..., None] * inv_freq
    lane = np.arange(LANES) % DA_QK_DIM
    hit = (lane[None, :] % half) == np.arange(half)[:, None]
    none = np.zeros_like(hit)
    sel = np.block([[hit & (lane < ROT_DIM), none, none],
                    [none, hit & (lane >= half) & (lane < ROT_DIM), hit & (lane < half)]])
    sign = np.concatenate([np.ones(2 * LANES), -np.ones(LANES)])
    one = np.concatenate([lane >= ROT_DIM, np.zeros(2 * LANES, bool)])
    cs = jnp.concatenate([jnp.cos(ang), jnp.sin(ang)], axis=-1)
    return (jnp.dot(cs, jnp.asarray(sel * sign, F32), precision=lax.Precision.HIGHEST)
            + jnp.asarray(one, F32))


def kernel(x, mem, positions, norm_mix, w_in, ssm_lam_re, ssm_lam_im, ssm_log_dt, ssm_b_re, ssm_b_im, ssm_c_re, ssm_c_im, ssm_d, ssm_w_glu, diff_lambda, diff_subln, w_branch, w_out, norm_cross, norm_mem, w_xq, w_xkv, w_xo, norm_mlp, w_up, w_down, norm_final):
    bsz, seq, _ = x.shape
    depth = w_in.shape[0]
    ts = min(64, seq)
    sb_tq = min(256, seq)
    da_tq = min(512, seq)
    tm = min(512, seq)

    rope = _rope_tables(positions)
    kv_all = _memkv(mem, norm_mem, w_xkv.astype(BF16))
    wa = w_in[:, :, :N_PROJ_A].astype(BF16)
    wg = w_in[:, :, N_PROJ_A:].astype(BF16)
    wglu, wbr, wout = ssm_w_glu.astype(BF16), w_branch.astype(BF16), w_out.astype(BF16)
    wxq, wxo, wup, wdown = w_xq.astype(BF16), w_xo.astype(BF16), w_up.astype(BF16), w_down.astype(BF16)

    for l in range(depth):
        ssm = _ssm_tables(ssm_lam_re[l], ssm_lam_im[l], ssm_log_dt[l], ssm_b_re[l], ssm_b_im[l],
                          ssm_c_re[l], ssm_c_im[l], bsz)
        qkv, yssm = _inproj_ssm(x, norm_mix[l], wa, rope, ssm, ssm_d[l], wglu, l, ts=ts)
        ysb = _sb_attention(qkv, tq=sb_tq, npair=4)
        lam_init = 0.8 - 0.6 * math.exp(-0.3 * l)
        yda = _da_attention(qkv, diff_lambda[l], diff_subln[l], lam_init, tq=da_tq, nhd=2)
        x = _merge_cross_mlp(x, yssm, ysb, yda, norm_mix[l], wg, wbr, wout, norm_cross[l], wxq, kv_all, wxo,
                             norm_mlp[l], wup, wdown, norm_final, l, final=(l == depth - 1), tm=tm)
    return x
```

```python
import functools
import math

import jax
import jax.numpy as jnp
import numpy as np
from jax import lax
from jax.experimental import pallas as pl
from jax.experimental.pallas import tpu as pltpu

F32 = jnp.float32
BF16 = jnp.bfloat16

D_MODEL = 1024
MEM_LEN = 256
RMS_EPS = 1e-6
SSM_WIDTH = 512
SSM_GROUP = 16
SSM_GROUPS = 32
SSM_STATE = 64
SB_HEADS = 8
SB_HEAD_DIM = 64
SB_WIDTH = 512
DA_HEADS = 4
DA_QK_DIM = 64
DA_V_DIM = 128
DA_QK_WIDTH = 512
DA_WIDTH = 512
ROT_DIM = 16
ROPE_THETA = 500000.0
N_BRANCH = 3
BRANCH_WIDTH = 512
XA_HEADS = 4
XA_HEAD_DIM = 128
XA_WIDTH = 512
MLP_HIDDEN = 4096

LANES = 128
SUBLANES = 8
VMEM_LIMIT_BYTES = 58 * 1024 * 1024

N_PROJ_A = SSM_WIDTH + 3 * SB_WIDTH + 2 * DA_QK_WIDTH + DA_WIDTH
N_QKV = N_PROJ_A - SSM_WIDTH
SBQ_BLK, SBK_BLK, SBV_BLK = 0, 4, 8
DAQ_BLK, DAK_BLK, DAV_BLK = 12, 16, 20

SSM_WIN = SSM_WIDTH // LANES
SSM_WIN_STATE = (SSM_GROUPS // SSM_WIN) * SSM_STATE

LOG2_E = 1.4426950408889634
SB_LOG2_REACH = 150.0
MASK_VALUE = -1e30
DA_ONES_ROWS = 16


def _rms(x, gain):
    ms = jnp.mean(x * x, axis=-1, keepdims=True)
    return x * lax.rsqrt(ms + RMS_EPS) * gain


def _dot(a, b):
    return jnp.dot(a, b, preferred_element_type=F32)


def _dot_nt(a, b):
    return lax.dot_general(a, b, (((1,), (1,)), ((), ())), preferred_element_type=F32)


def _params(*sem):
    return pltpu.CompilerParams(dimension_semantics=sem, vmem_limit_bytes=VMEM_LIMIT_BYTES)


def _const_spec(shape):
    nd = len(shape)
    return pl.BlockSpec(shape, lambda *_: (0,) * nd, pipeline_mode=pl.Buffered(1))


def _layer_spec(shape, layer):
    nd = len(shape)
    return pl.BlockSpec((None,) + tuple(shape), lambda *_: (layer,) + (0,) * nd, pipeline_mode=pl.Buffered(1))


def _memkv_kernel(mem_ref, g_ref, w_ref, o_ref):
    h = _rms(mem_ref[0], g_ref[0]).astype(BF16)
    o_ref[0, 0] = _dot(h, w_ref[0]).astype(BF16)


def _memkv(mem, norm_mem, w_xkv_bf):
    depth = w_xkv_bf.shape[0]
    bsz = mem.shape[0]
    return pl.pallas_call(
        _memkv_kernel,
        grid=(depth, bsz),
        in_specs=[
            pl.BlockSpec((1, MEM_LEN, D_MODEL), lambda l, b: (b, 0, 0)),
            pl.BlockSpec((1, 1, D_MODEL), lambda l, b: (l, 0, 0)),
            pl.BlockSpec((1, D_MODEL, 2 * XA_WIDTH), lambda l, b: (l, 0, 0)),
        ],
        out_specs=pl.BlockSpec((1, 1, MEM_LEN, 2 * XA_WIDTH), lambda l, b: (l, b, 0, 0)),
        out_shape=jax.ShapeDtypeStruct((depth, bsz, MEM_LEN, 2 * XA_WIDTH), BF16),
        compiler_params=_params("arbitrary", "arbitrary"),
        name="memkv",
    )(mem, norm_mem.reshape(depth, 1, D_MODEL), w_xkv_bf)


def _inproj_ssm_kernel(x_ref, g_ref, wa_ref, rope_ref, bw_ref, lr_ref, li_ref,
                       cw_ref, d_ref, wglu_ref, qkv_ref, yssm_ref, u_tb, bux, st, y_tb, *, ts, bsz):
    @pl.when(pl.program_id(0) == 0)
    def _():
        st[...] = jnp.zeros_like(st)

    m = ts * bsz
    h = _rms(x_ref[...].reshape(m, D_MODEL), g_ref[...]).astype(BF16)

    def proj(c0, width=SSM_WIDTH):
        return _dot(h, wa_ref[:, c0:c0 + width])

    def put(dst, val):
        qkv_ref[:, :, dst:dst + val.shape[1]] = val.astype(BF16).reshape(bsz, ts, val.shape[1])

    u = proj(0)
    for b in range(bsz):
        for w in range(SSM_WIN):
            u_tb[w, pl.ds(b, ts, stride=bsz), :] = u[ts * b:ts * (b + 1), LANES * w:LANES * (w + 1)]

    wst = 2 * SSM_WIN_STATE
    for w in range(SSM_WIN):
        bux[:, wst * w:wst * (w + 1)] = _dot(u_tb[w].astype(BF16), bw_ref[w])

    nhalf = 2 * SSM_WIN
    hs = SSM_WIN_STATE
    state = [st[:, hs * k:hs * (k + 1)] for k in range(nhalf)]
    for t in range(ts):
        rows = slice(bsz * t, bsz * (t + 1))
        for w in range(SSM_WIN):
            xr, xi = state[2 * w], state[2 * w + 1]
            lr = lr_ref[:, hs * w:hs * (w + 1)]
            li = li_ref[:, hs * w:hs * (w + 1)]
            cr = slice(hs * 2 * w, hs * (2 * w + 1))
            ci = slice(hs * (2 * w + 1), hs * (2 * w + 2))
            state[2 * w] = lr * xr - li * xi + bux[rows, cr]
            state[2 * w + 1] = lr * xi + li * xr + bux[rows, ci]
            bux[rows, cr] = state[2 * w]
            bux[rows, ci] = state[2 * w + 1]
    for k in range(nhalf):
        st[:, hs * k:hs * (k + 1)] = state[k]

    o = SSM_WIDTH
    put(0, proj(o) * (SB_HEAD_DIM ** -0.5 * LOG2_E))
    put(SB_WIDTH, proj(o + SB_WIDTH))
    put(2 * SB_WIDTH, proj(o + 2 * SB_WIDTH))
    rope = rope_ref[...].reshape(m, 3 * LANES)
    cosf, sina, sinb = rope[:, :LANES], rope[:, LANES:2 * LANES], rope[:, 2 * LANES:]
    src = o + 3 * SB_WIDTH
    dst = 3 * SB_WIDTH
    for scale in (DA_QK_DIM ** -0.5 * LOG2_E, 1.0):
        t4 = proj(src)
        for hh in range(DA_HEADS):
            t = t4[:, LANES * hh:LANES * (hh + 1)]
            r = (t * cosf + pltpu.roll(t, ROT_DIM // 2, 1) * sina
                 + pltpu.roll(t, LANES - ROT_DIM // 2, 1) * sinb)
            put(dst + LANES * hh, r * scale)
        src += DA_QK_WIDTH
        dst += DA_QK_WIDTH
    put(dst, proj(src))

    ys = []
    for w in range(SSM_WIN):
        xw = bux[:, wst * w:wst * (w + 1)].astype(BF16)
        yw = _dot(xw, cw_ref[w])
        yw = yw + d_ref[:, LANES * w:LANES * (w + 1)] * u_tb[w]
        ys.append(jax.nn.gelu(yw).astype(BF16))
    glu = _dot(jnp.concatenate(ys, axis=1), wglu_ref[...])
    for w in range(SSM_WIN):
        sl = slice(LANES * w, LANES * (w + 1))
        y_tb[w] = glu[:, sl] * jax.nn.sigmoid(glu[:, SSM_WIDTH + LANES * w:SSM_WIDTH + LANES * (w + 1)])
    for b in range(bsz):
        for w in range(SSM_WIN):
            yssm_ref[b, :, LANES * w:LANES * (w + 1)] = y_tb[w, pl.ds(b, ts, stride=bsz), :].astype(BF16)


def _inproj_ssm(x, gain, wa, rope, ssm, d_skip, w_glu, layer, *, ts):
    bsz, seq, _ = x.shape
    assert bsz == SUBLANES and seq % ts == 0
    bw, lr, li, cw = ssm
    m = ts * bsz
    tile = lambda width: pl.BlockSpec((bsz, ts, width), lambda i: (0, i, 0))
    return pl.pallas_call(
        functools.partial(_inproj_ssm_kernel, ts=ts, bsz=bsz),
        grid=(seq // ts,),
        in_specs=[
            tile(D_MODEL),
            _const_spec((1, D_MODEL)),
            _layer_spec((D_MODEL, N_PROJ_A), layer),
            tile(3 * LANES),
            _const_spec(bw.shape), _const_spec(lr.shape), _const_spec(li.shape), _const_spec(cw.shape),
            _const_spec((1, SSM_WIDTH)),
            _layer_spec((SSM_WIDTH, 2 * SSM_WIDTH), layer),
        ],
        out_specs=[tile(N_QKV), tile(SSM_WIDTH)],
        out_shape=[jax.ShapeDtypeStruct((bsz, seq, N_QKV), BF16),
                   jax.ShapeDtypeStruct((bsz, seq, SSM_WIDTH), BF16)],
        scratch_shapes=[
            pltpu.VMEM((SSM_WIN, m, LANES), F32),
            pltpu.VMEM((m, 2 * SSM_WIN * SSM_WIN_STATE), F32),
            pltpu.VMEM((bsz, 2 * SSM_WIN * SSM_WIN_STATE), F32),
            pltpu.VMEM((SSM_WIN, m, LANES), F32),
        ],
        compiler_params=_params("arbitrary"),
        name="inproj_ssm",
    )(x, gain.reshape(1, D_MODEL), wa, rope, bw, lr, li, cw,
      d_skip.reshape(1, SSM_WIDTH), w_glu)


def _sb_kernel(q_ref, k_ref, v_ref, o_ref, vt, *, tq, npair):
    i = pl.program_id(2)
    tk = tq
    nhead = 2 * npair
    blk = lambda p: slice(LANES * p, LANES * (p + 1))

    @pl.when(i == 0)
    def _():
        for p in range(npair):
            for c in range(vt.shape[1]):
                vt[p, c] = v_ref[0, tk * c:tk * (c + 1), blk(p)].astype(F32).T.astype(BF16)

    lane = lax.broadcasted_iota(jnp.int32, (tq, LANES), 1)
    qs = []
    for p in range(npair):
        q2 = q_ref[0, :, blk(p)]
        zero = jnp.zeros_like(q2)
        qs += [jnp.where(lane < SB_HEAD_DIM, q2, zero), jnp.where(lane >= SB_HEAD_DIM, q2, zero)]
    key = lax.broadcasted_iota(jnp.int32, (tk, tq), 0)
    qry = lax.broadcasted_iota(jnp.int32, (tk, tq), 1)
    strict = key < qry
    tri = (lax.broadcasted_iota(jnp.int32, (tk, tk), 1)
           > lax.broadcasted_iota(jnp.int32, (tk, tk), 0)).astype(BF16)

    def blocks(js, masks, runs):
        rows = [pl.ds(pl.multiple_of(j * tk, tk), tk) for j in js]
        z = [[_dot_nt(k_ref[0, r, blk(a // 2)], qs[a]) for a in range(nhead)] for r in rows]
        sp, hi = [], []
        for n, masked in enumerate(masks):
            if masked:
                z[n] = [jnp.where(strict, t, MASK_VALUE) for t in z[n]]
            row = []
            for a in range(nhead):
                t = z[n][a]
                row.append(jnp.maximum(t, 0.0) + jnp.log2(1.0 + jnp.exp2(-jnp.abs(t))))
            sp.append(row)
            hi.append([t.astype(BF16) for t in row])
        cum = [[_dot(tri, hi[n][a]) for a in range(nhead)] for n in range(len(js))]
        w = []
        runs = list(runs)
        for n in range(len(js)):
            row = []
            for a in range(nhead):
                row.append(jnp.exp2(z[n][a] - sp[n][a] - (cum[n][a] + runs[a])).astype(BF16))
                runs[a] = runs[a] + jnp.sum(sp[n][a], axis=0, keepdims=True)
            w.append(row)
        pvs = [[_dot(vt[a // 2, j], w[n][a]) for a in range(nhead)] for n, j in enumerate(js)]
        return pvs, runs

    zrun = jnp.zeros((1, tq), F32)
    (accs, prev), runs = blocks((i, jnp.maximum(i - 1, 0)), (True, False), [zrun] * nhead)
    live = i > 0
    accs = [a + jnp.where(live, p, 0.0) for a, p in zip(accs, prev)]

    def cond(c):
        j, runs, _ = c
        return jnp.logical_and(j >= 0, jnp.min(functools.reduce(jnp.minimum, runs)) < SB_LOG2_REACH)

    def body(c):
        j, runs, accs = c
        (pvs,), runs = blocks((j,), (False,), runs)
        return j - 1, tuple(runs), tuple(a + p for a, p in zip(accs, pvs))

    _, _, accs = lax.while_loop(cond, body, (i - 2, tuple(runs), tuple(accs)))
    dim = lax.broadcasted_iota(jnp.int32, (LANES, tq), 0)
    for p in range(npair):
        o_ref[0, :, blk(p)] = jnp.where(dim < SB_HEAD_DIM, accs[2 * p], accs[2 * p + 1]).T.astype(BF16)


def _sb_attention(qkv, *, tq, npair):
    bsz, seq, _ = qkv.shape
    width = LANES * npair
    ngroup = SB_WIDTH // width
    col = lambda first: (lambda b, g, i: (b, 0, first * LANES // width + g))
    return pl.pallas_call(
        functools.partial(_sb_kernel, tq=tq, npair=npair),
        grid=(bsz, ngroup, seq // tq),
        in_specs=[
            pl.BlockSpec((1, tq, width), lambda b, g, i: (b, i, SBQ_BLK * LANES // width + g)),
            pl.BlockSpec((1, seq, width), col(SBK_BLK)),
            pl.BlockSpec((1, seq, width), col(SBV_BLK)),
        ],
        out_specs=pl.BlockSpec((1, tq, width), lambda b, g, i: (b, i, g)),
        out_shape=jax.ShapeDtypeStruct((bsz, seq, SB_WIDTH), BF16),
        scratch_shapes=[pltpu.VMEM((npair, seq // tq, LANES, tq), BF16)],
        compiler_params=_params("arbitrary", "arbitrary", "arbitrary"),
        name="sb_attn",
    )(qkv, qkv, qkv)


def _da_kernel(q_ref, k_ref, v_ref, lam_ref, g_ref, o_ref, vt, sa, sb, acc, *, tq, th, nhd, lam_init):
    i = pl.program_id(2)
    nchain = 2 * nhd
    blk = lambda hh: slice(LANES * hh, LANES * (hh + 1))

    @pl.when(i == 0)
    def _():
        for hh in range(nhd):
            for c in range(vt.shape[1]):
                vt[hh, c, :DA_V_DIM, :] = v_ref[0, th * c:th * (c + 1), blk(hh)].astype(F32).T.astype(BF16)
                vt[hh, c, DA_V_DIM:, :] = jnp.ones((DA_ONES_ROWS, th), BF16)

    lane = lax.broadcasted_iota(jnp.int32, (tq, LANES), 1)

    def queries(tile):
        rows = pl.ds(pl.multiple_of(tile * tq, tq), tq)
        out = []
        for hh in range(nhd):
            q = q_ref[0, rows, blk(hh)]
            zero = jnp.zeros_like(q)
            out += [jnp.where(lane < DA_QK_DIM, q, zero), jnp.where(lane >= DA_QK_DIM, q, zero)]
        return out

    qs = queries(i)

    def scores_to(buf, kblk, qs=qs):
        rows = pl.ds(pl.multiple_of(kblk * th, th), th)
        for x in range(nchain):
            buf[x] = _dot_nt(k_ref[0, rows, blk(x // 2)], qs[x])

    @pl.when(i == 0)
    def _():
        scores_to(sa, 0)

    def update(buf, kblk, stats, masked):
        if masked:
            key = lax.broadcasted_iota(jnp.int32, (th, tq), 0) + kblk * th
            qry = lax.broadcasted_iota(jnp.int32, (th, tq), 1) + i * tq
            causal = key <= qry
        halves = [slice(th * hf, th * (hf + 1)) for hf in range(tq // th)]
        new, ps = [], []
        for x in range(nchain):
            mns, row = [], []
            for sl in halves:
                s = buf[x, :, sl]
                if masked:
                    s = jnp.where(causal[:, sl], s, MASK_VALUE)
                mn = jnp.maximum(stats[x][:, sl], jnp.max(s, axis=0, keepdims=True))
                row.append(jnp.exp2(s - mn).astype(BF16))
                mns.append(mn)
            ps.append(row)
            new.append(jnp.concatenate(mns, axis=1))
        for x in range(nchain):
            alpha = jnp.exp2(stats[x] - new[x])
            for sl, p in zip(halves, ps[x]):
                acc[x, :, sl] = alpha[:, sl] * acc[x, :, sl] + _dot(vt[x // 2, kblk], p)
        return tuple(new)

    def pair(j, stats, masked):
        scores_to(sb, 2 * j + 1)
        stats = update(sa, 2 * j, stats, masked)
        if masked:
            scores_to(sa, 0, queries(jnp.minimum(i + 1, q_ref.shape[1] // tq - 1)))
        else:
            scores_to(sa, 2 * j + 2)
        return update(sb, 2 * j + 1, stats, masked)

    acc[...] = jnp.zeros_like(acc)
    init = tuple(jnp.full((1, tq), MASK_VALUE, F32) for _ in range(nchain))
    stats = lax.fori_loop(0, i, lambda j, st: pair(j, st, False), init)
    pair(i, stats, True)

    lv = lam_ref[...]
    lam = (jnp.exp(jnp.sum(lv[0:1] * lv[1:2], axis=1, keepdims=True))
           - jnp.exp(jnp.sum(lv[2:3] * lv[3:4], axis=1, keepdims=True)) + lam_init)
    for hh in range(nhd):
        l1 = acc[2 * hh, DA_V_DIM:DA_V_DIM + 1, :]
        l2 = acc[2 * hh + 1, DA_V_DIM:DA_V_DIM + 1, :]
        ot = acc[2 * hh, :DA_V_DIM, :] * (1.0 / l1) - acc[2 * hh + 1, :DA_V_DIM, :] * (lam / l2)
        ms = jnp.mean(ot * ot, axis=0, keepdims=True)
        ot = ot * (lax.rsqrt(ms + RMS_EPS) * (1.0 - lam_init)) * g_ref[...]
        o_ref[0, :, blk(hh)] = ot.T.astype(BF16)


def _da_attention(qkv, diff_lambda, diff_subln, lam_init, *, tq, nhd):
    bsz, seq, _ = qkv.shape
    th = tq // 2
    width = LANES * nhd
    assert seq % tq == 0 and DA_HEADS % nhd == 0
    col = lambda first: (lambda b, g, i: (b, 0, first * LANES // width + g))
    return pl.pallas_call(
        functools.partial(_da_kernel, tq=tq, th=th, nhd=nhd, lam_init=lam_init),
        grid=(bsz, DA_HEADS // nhd, seq // tq),
        in_specs=[
            pl.BlockSpec((1, seq, width), col(DAQ_BLK)),
            pl.BlockSpec((1, seq, width), col(DAK_BLK)),
            pl.BlockSpec((1, seq, width), col(DAV_BLK)),
            pl.BlockSpec((4, DA_QK_DIM), lambda b, g, i: (0, 0)),
            pl.BlockSpec((DA_V_DIM, 1), lambda b, g, i: (0, 0)),
        ],
        out_specs=pl.BlockSpec((1, tq, width), lambda b, g, i: (b, i, g)),
        out_shape=jax.ShapeDtypeStruct((bsz, seq, DA_WIDTH), BF16),
        scratch_shapes=[pltpu.VMEM((nhd, seq // th, DA_V_DIM + DA_ONES_ROWS, th), BF16),
                        pltpu.VMEM((2 * nhd, th, tq), F32), pltpu.VMEM((2 * nhd, th, tq), F32),
                        pltpu.VMEM((2 * nhd, DA_V_DIM + DA_ONES_ROWS, tq), F32)],
        compiler_params=_params("arbitrary", "arbitrary", "arbitrary"),
        name="da_attn",
    )(qkv, qkv, qkv, diff_lambda, diff_subln.reshape(DA_V_DIM, 1))


def _merge_cross_mlp_kernel(x_ref, yssm_ref, ysb_ref, yda_ref, gmix_ref, wg_ref, wbr_ref, wout_ref,
                            gcross_ref, wxq_ref, kv_ref, wxo_ref, gmlp_ref, wup_ref, wdown_ref, gfin_ref,
                            o_ref, *, final, chunk):
    x = x_ref[0]
    h = _rms(x, gmix_ref[...]).astype(BF16)
    merged = None
    for n, y_ref in enumerate((yssm_ref, ysb_ref, yda_ref)):
        gate = jax.nn.sigmoid(_dot(h, wg_ref[:, D_MODEL * n:D_MODEL * (n + 1)]))
        term = gate * _dot(y_ref[0], wbr_ref[n])
        merged = term if merged is None else merged + term
    x1 = x + _dot(merged.astype(BF16), wout_ref[...])

    hx = _rms(x1, gcross_ref[...]).astype(BF16)
    xq = _dot(hx, wxq_ref[...]).astype(BF16)
    kv = kv_ref[0]
    heads = []
    for hh in range(XA_HEADS):
        sl = slice(XA_HEAD_DIM * hh, XA_HEAD_DIM * (hh + 1))
        s = _dot_nt(xq[:, sl], kv[:, sl]) * XA_HEAD_DIM ** -0.5
        p = jnp.exp(s - jnp.max(s, axis=1, keepdims=True))
        vh = kv[:, XA_WIDTH + XA_HEAD_DIM * hh:XA_WIDTH + XA_HEAD_DIM * (hh + 1)]
        heads.append((_dot(p.astype(BF16), vh) / jnp.sum(p, axis=1, keepdims=True)).astype(BF16))
    x2 = x1 + _dot(jnp.concatenate(heads, axis=1), wxo_ref[...])

    hm = _rms(x2, gmlp_ref[...]).astype(BF16)
    acc = x2
    for c in range(MLP_HIDDEN // chunk):
        up = _dot(hm, wup_ref[:, chunk * c:chunk * (c + 1)])
        act = jnp.square(jnp.maximum(up, 0.0)).astype(BF16)
        acc = acc + _dot(act, wdown_ref[chunk * c:chunk * (c + 1), :])
    if final:
        acc = _rms(acc, gfin_ref[...])
    o_ref[0] = acc


def _merge_cross_mlp(x, yssm, ysb, yda, gmix, wg, wbr, wout, gcross, wxq, kv, wxo, gmlp, wup, wdown, gfin,
                     layer, *, final, tm):
    bsz, seq, _ = x.shape
    tile = lambda width: pl.BlockSpec((1, tm, width), lambda b, i: (b, i, 0))
    gain = lambda g: g.reshape(1, D_MODEL)
    return pl.pallas_call(
        functools.partial(_merge_cross_mlp_kernel, final=final, chunk=1024),
        grid=(bsz, seq // tm),
        in_specs=[
            tile(D_MODEL), tile(BRANCH_WIDTH), tile(BRANCH_WIDTH), tile(BRANCH_WIDTH),
            _const_spec((1, D_MODEL)),
            _layer_spec((D_MODEL, N_BRANCH * D_MODEL), layer),
            _layer_spec((N_BRANCH, BRANCH_WIDTH, D_MODEL), layer),
            _layer_spec((D_MODEL, D_MODEL), layer),
            _const_spec((1, D_MODEL)),
            _layer_spec((D_MODEL, XA_WIDTH), layer),
            pl.BlockSpec((None, 1, MEM_LEN, 2 * XA_WIDTH), lambda b, i: (layer, b, 0, 0)),
            _layer_spec((XA_WIDTH, D_MODEL), layer),
            _const_spec((1, D_MODEL)),
            _layer_spec((D_MODEL, MLP_HIDDEN), layer),
            _layer_spec((MLP_HIDDEN, D_MODEL), layer),
            _const_spec((1, D_MODEL)),
        ],
        out_specs=tile(D_MODEL),
        out_shape=jax.ShapeDtypeStruct((bsz, seq, D_MODEL), F32),
        compiler_params=_params("parallel", "arbitrary"),
        name="merge_cross_mlp",
    )(x, yssm, ysb, yda, gain(gmix), wg, wbr, wout, gain(gcross), wxq, kv, wxo, gain(gmlp), wup, wdown, gain(gfin))


def _ssm_tables(lam_re, lam_im, log_dt, b_re, b_im, c_re, c_im, bsz):
    dt = jnp.exp(log_dt)[:, None]
    mag = jnp.exp(lam_re * dt)
    lbr = mag * jnp.cos(lam_im * dt)
    lbi = mag * jnp.sin(lam_im * dt)
    den = lam_re * lam_re + lam_im * lam_im
    fr = ((lbr - 1.0) * lam_re + lbi * lam_im) / den
    fi = (lbi * lam_re - (lbr - 1.0) * lam_im) / den
    bbr = fr[..., None] * b_re - fi[..., None] * b_im
    bbi = fr[..., None] * b_im + fi[..., None] * b_re
    gl = SSM_GROUPS // SSM_WIN
    eye = jnp.eye(gl, dtype=F32)
    shp = (SSM_WIN, gl, SSM_STATE, SSM_GROUP)
    to_b = lambda t: jnp.einsum("wgpc,gh->wgchp", t.reshape(shp), eye)
    bw = jnp.stack([to_b(bbr), to_b(bbi)], axis=3).reshape(SSM_WIN, LANES, 2 * SSM_WIN_STATE)
    shc = (SSM_WIN, gl, SSM_GROUP, SSM_STATE)
    to_c = lambda t: jnp.einsum("wgcp,gh->wgphc", t.reshape(shc), eye)
    cw = jnp.stack([to_c(c_re), -to_c(c_im)], axis=1).reshape(SSM_WIN, 2 * SSM_WIN_STATE, LANES)
    lr = jnp.broadcast_to(lbr.reshape(1, -1), (bsz, SSM_GROUPS * SSM_STATE))
    li = jnp.broadcast_to(lbi.reshape(1, -1), (bsz, SSM_GROUPS * SSM_STATE))
    return bw.astype(BF16), lr, li, cw.astype(BF16)


def _rope_tables(positions):
    half = ROT_DIM // 2
    inv_freq = ROPE_THETA ** (-jnp.arange(0, ROT_DIM, 2, dtype=F32) / ROT_DIM)
    ang = positions.astype(F32)[..., None] * inv_freq
    lane = np.arange(LANES) % DA_QK_DIM
    hit = (lane[None, :] % half) == np.arange(half)[:, None]
    none = np.zeros_like(hit)
    sel = np.block([[hit & (lane < ROT_DIM), none, none],
                    [none, hit & (lane >= half) & (lane < ROT_DIM), hit & (lane < half)]])
    sign = np.concatenate([np.ones(2 * LANES), -np.ones(LANES)])
    one = np.concatenate([lane >= ROT_DIM, np.zeros(2 * LANES, bool)])
    cs = jnp.concatenate([jnp.cos(ang), jnp.sin(ang)], axis=-1)
    return (jnp.dot(cs, jnp.asarray(sel * sign, F32), precision=lax.Precision.HIGHEST)
            + jnp.asarray(one, F32))


def kernel(x, mem, positions, norm_mix, w_in, ssm_lam_re, ssm_lam_im, ssm_log_dt, ssm_b_re, ssm_b_im, ssm_c_re, ssm_c_im, ssm_d, ssm_w_glu, diff_lambda, diff_subln, w_branch, w_out, norm_cross, norm_mem, w_xq, w_xkv, w_xo, norm_mlp, w_up, w_down, norm_final):
    bsz, seq, _ = x.shape
    depth = w_in.shape[0]
    ts = min(64, seq)
    sb_tq = min(256, seq)
    da_tq = min(512, seq)
    tm = min(512, seq)

    rope = _rope_tables(positions)
    kv_all = _memkv(mem, norm_mem, w_xkv.astype(BF16))
    wa = w_in[:, :, :N_PROJ_A].astype(BF16)
    wg = w_in[:, :, N_PROJ_A:].astype(BF16)
    wglu, wbr, wout = ssm_w_glu.astype(BF16), w_branch.astype(BF16), w_out.astype(BF16)
    wxq, wxo, wup, wdown = w_xq.astype(BF16), w_xo.astype(BF16), w_up.astype(BF16), w_down.astype(BF16)

    for l in range(depth):
        ssm = _ssm_tables(ssm_lam_re[l], ssm_lam_im[l], ssm_log_dt[l], ssm_b_re[l], ssm_b_im[l],
                          ssm_c_re[l], ssm_c_im[l], bsz)
        qkv, yssm = _inproj_ssm(x, norm_mix[l], wa, rope, ssm, ssm_d[l], wglu, l, ts=ts)
        ysb = _sb_attention(qkv, tq=sb_tq, npair=4)
        lam_init = 0.8 - 0.6 * math.exp(-0.3 * l)
        yda = _da_attention(qkv, diff_lambda[l], diff_subln[l], lam_init, tq=da_tq, nhd=2)
        x = _merge_cross_mlp(x, yssm, ysb, yda, norm_mix[l], wg, wbr, wout, norm_cross[l], wxq, kv_all, wxo,
                             norm_mlp[l], wup, wdown, norm_final, l, final=(l == depth - 1), tm=tm)
    return x
```
